```python
import jax, jax.numpy as jnp
from jax import lax
import numpy as np

D_MODEL = 2048
BATCH = 8
SEQ = 4096
DEPTH = 4

CHUNK = 64
N_MIXERS = 4
EPS = 1e-6
N_MOD = 6

CONV_WIDTH = 31

GLA_HEADS = 4
GLA_DK = D_MODEL // (2 * GLA_HEADS)
GLA_DV = D_MODEL // GLA_HEADS
GLA_GATE_RANK = 16
GLA_GATE_TEMP = 16.0

RET_HEADS = 8
RET_DK = D_MODEL // RET_HEADS
RET_DV = 2 * RET_DK
ROPE_BASE = 10000.0

SGU_CHUNK = 128
SGU_HEADS = 8
SGU_WIDTH = 2 * D_MODEL
SGU_HEAD_DIM = SGU_WIDTH // SGU_HEADS

FFN_HIDDEN = 5632
FFN_CONV_WIDTH = 3

GLA_IN = 2 * GLA_HEADS * GLA_DK + 2 * GLA_HEADS * GLA_DV + GLA_GATE_RANK
RET_IN = 2 * RET_HEADS * RET_DK + 2 * RET_HEADS * RET_DV

kernel_name = 'hybrid_streaming_encoder_interleaved'


def _n_uses(m):
    return (DEPTH - m + N_MIXERS - 1) // N_MIXERS


def _rms(x):
    x32 = x.astype(jnp.float32)
    return x32 * lax.rsqrt(jnp.mean(jnp.square(x32), axis=-1, keepdims=True) + EPS)


def _rms_norm(x, g):
    return (_rms(x) * g.astype(jnp.float32)).astype(x.dtype)


def _layer_norm(x, g, b):
    x32 = x.astype(jnp.float32)
    mu = jnp.mean(x32, axis=-1, keepdims=True)
    var = jnp.mean(jnp.square(x32 - mu), axis=-1, keepdims=True)
    y = (x32 - mu) * lax.rsqrt(var + EPS) * g.astype(jnp.float32) + b.astype(jnp.float32)
    return y.astype(x.dtype)


def _causal_dwconv(x, w, b):
    k = w.shape[0]
    xp = jnp.pad(x, ((0, 0), (k - 1, 0), (0, 0)))
    y = lax.conv_general_dilated(xp, w[:, None, :].astype(x.dtype), (1,), 'VALID',
                                 dimension_numbers=('NWC', 'WIO', 'NWC'),
                                 feature_group_count=x.shape[-1])
    return y + b


def _to_chunks(t):
    B, L, H, d = t.shape
    return t.reshape(B, L // CHUNK, CHUNK, H, d).transpose(1, 0, 3, 2, 4)


def _from_chunks(t):
    nC, B, H, C, d = t.shape
    return t.transpose(1, 0, 3, 2, 4).reshape(B, nC * C, H, d)


def _rotary(t, pos):
    half = t.shape[-1] // 2
    inv_freq = ROPE_BASE ** (-jnp.arange(half, dtype=jnp.float32) / half)
    ang = pos.astype(jnp.float32)[:, None] * inv_freq
    cos = jnp.cos(ang)[None, :, None, :]
    sin = jnp.sin(ang)[None, :, None, :]
    t1, t2 = t[..., :half], t[..., half:]
    return jnp.concatenate([t1 * cos - t2 * sin, t1 * sin + t2 * cos], axis=-1)


def _conv_module(h, w_in, dw_w, dw_b, ln_g, ln_b, w_out):
    a, gate = jnp.split(h @ w_in, 2, axis=-1)
    z = _causal_dwconv(a * jax.nn.sigmoid(gate), dw_w, dw_b)
    z = jax.nn.silu(_layer_norm(z, ln_g, ln_b))
    return z @ w_out


def _gla_mixer(h, w_in, w_gate, b_gate, norm_g, w_out):
    B, L, _ = h.shape
    f32 = jnp.float32
    hk, hv = GLA_HEADS * GLA_DK, GLA_HEADS * GLA_DV
    q, k, v, r, g_low = jnp.split(h @ w_in, [hk, 2 * hk, 2 * hk + hv, 2 * hk + 2 * hv], axis=-1)
    log_a = jax.nn.log_sigmoid((g_low @ w_gate + b_gate).astype(f32)) / GLA_GATE_TEMP
    q = _to_chunks(q.astype(f32).reshape(B, L, GLA_HEADS, GLA_DK)) * GLA_DK ** -0.5
    k = _to_chunks(k.astype(f32).reshape(B, L, GLA_HEADS, GLA_DK))
    v = _to_chunks(v.astype(f32).reshape(B, L, GLA_HEADS, GLA_DV))
    b = jnp.cumsum(_to_chunks(log_a.reshape(B, L, GLA_HEADS, GLA_DK)), axis=3)
    b_last = b[..., -1:, :]
    e_pos, e_neg = jnp.exp(b), jnp.exp(-b)
    q_fwd, k_fwd = q * e_pos, k * e_neg
    q_bwd, k_bwd = q * e_neg, k * e_pos
    s_fwd = jnp.einsum('nbhtd,nbhsd->nbhts', q_fwd, k_fwd)
    s_bwd = jnp.einsum('nbhtd,nbhsd->nbhts', q_bwd, k_bwd)
    causal = jnp.tril(jnp.ones((CHUNK, CHUNK), dtype=bool))
    scores = jnp.where(causal, s_fwd, s_bwd)
    o_intra = jnp.einsum('nbhts,nbhsv->nbhtv', scores, v)
    k_state = k * jnp.exp(b_last - b)
    decay = jnp.exp(b_last[..., 0, :])

    def step(state, xs):
        q_c, k_c, v_c, d_c = xs
        o = jnp.einsum('bhtd,bhdv->bhtv', q_c, state)
        state = state * d_c[..., None] + jnp.einsum('bhsd,bhsv->bhdv', k_c, v_c)
        return state, o

    state0 = jnp.zeros((B, GLA_HEADS, GLA_DK, GLA_DV), f32)
    _, o_inter = lax.scan(step, state0, (q_fwd, k_state, v, decay))
    o = _rms(_from_chunks(o_intra + o_inter)) * norm_g.astype(f32)
    o = o.reshape(B, L, hv).astype(h.dtype) * jax.nn.silu(r)
    return o @ w_out


def _retention_mixer(h, w_in, w_out):
    B, L, _ = h.shape
    f32 = jnp.float32
    hk, hv = RET_HEADS * RET_DK, RET_HEADS * RET_DV
    q, k, v, g = jnp.split(h @ w_in, [hk, 2 * hk, 2 * hk + hv], axis=-1)
    pos = jnp.arange(L)
    q = _rotary(q.astype(f32).reshape(B, L, RET_HEADS, RET_DK), pos)
    k = _rotary(k.astype(f32).reshape(B, L, RET_HEADS, RET_DK), pos) * RET_DK ** -0.5
    v = v.astype(f32).reshape(B, L, RET_HEADS, RET_DV)
    q, k, v = _to_chunks(q), _to_chunks(k), _to_chunks(v)
    log_g = jnp.log1p(-jnp.exp2(-5.0 - jnp.arange(RET_HEADS, dtype=f32)))
    idx = jnp.arange(CHUNK, dtype=f32)
    d_intra = jnp.exp(log_g[:, None, None] * jnp.abs(idx[:, None] - idx[None, :]))
    q_in = q * jnp.exp(log_g[:, None] * (idx + 1.0))[:, :, None]
    k_in = k * jnp.exp(log_g[:, None] * (CHUNK - 1.0 - idx))[:, :, None]
    chunk_decay = jnp.exp(log_g * CHUNK)[None, :, None, None]
    scores = jnp.einsum('nbhtd,nbhsd->nbhts', q, k) * d_intra
    o_intra = jnp.einsum('nbhts,nbhsv->nbhtv', scores, v)

    def step(state, xs):
        q_c, k_c, v_c = xs
        o = jnp.einsum('bhtd,bhdv->bhtv', q_c, state)
        state = state * chunk_decay + jnp.einsum('bhsd,bhsv->bhdv', k_c, v_c)
        return state, o

    state0 = jnp.zeros((B, RET_HEADS, RET_DK, RET_DV), f32)
    _, o_inter = lax.scan(step, state0, (q_in, k_in, v))
    o = _rms(_from_chunks(o_intra + o_inter))
    o = o.reshape(B, L, hv).astype(h.dtype) * jax.nn.silu(g)
    return o @ w_out


def _sgu_mixer(h, w_in, ln_g, ln_b, w_s, b_s, w_out):
    B, L, _ = h.shape
    u, v = jnp.split(jax.nn.gelu(h @ w_in, approximate=False), 2, axis=-1)
    v = _layer_norm(v, ln_g, ln_b).reshape(B, L // SGU_CHUNK, SGU_CHUNK, SGU_HEADS, SGU_HEAD_DIM)
    p = jnp.arange(SGU_CHUNK)
    mask = (p[None, :] // CHUNK) <= (p[:, None] // CHUNK)
    w = jnp.where(mask[None], w_s, 0)
    sv = jnp.einsum('hij,bnjhd->bnihd', w, v) + b_s.T[None, None, :, :, None]
    return (u * sv.reshape(B, L, SGU_WIDTH)) @ w_out


def _conv_ffn(h, w_in, conv_w, conv_b, w_out):
    a, b = jnp.split(h @ w_in, 2, axis=-1)
    a = _causal_dwconv(a, conv_w, conv_b)
    return (jax.nn.silu(a) * b) @ w_out


def setup_inputs(seed: int = 0) -> dict:
    key = jax.random.key(seed)
    ks = iter(jax.random.split(key, 32))
    f32 = jnp.float32

    def nrm(shape, scale):
        return jax.random.normal(next(ks), shape, f32) * scale

    D, F = D_MODEL, FFN_HIDDEN
    nA, nB, nC, nD = (_n_uses(m) for m in range(N_MIXERS))
    return {
        'x': nrm((BATCH, SEQ, D), 1.0),
        'c': nrm((BATCH, D), 1.0),
        'ada_w': nrm((DEPTH, D, N_MOD * D), 0.5 * D ** -0.5),
        'ada_b': nrm((DEPTH, N_MOD * D), 0.02),
        'norm_g': 1.0 + nrm((DEPTH, 4, D), 0.05),
        'ffn_w_in': nrm((DEPTH, D, 2 * F), D ** -0.5),
        'ffn_conv_w': nrm((DEPTH, FFN_CONV_WIDTH, F), FFN_CONV_WIDTH ** -0.5),
        'ffn_conv_b': nrm((DEPTH, F), 0.02),
        'ffn_w_out': nrm((DEPTH, F, D), F ** -0.5),
        'cm_w_in': nrm((nA, D, 2 * D), D ** -0.5),
        'cm_dw_w': nrm((nA, CONV_WIDTH, D), CONV_WIDTH ** -0.5),
        'cm_dw_b': nrm((nA, D), 0.02),
        'cm_ln_g': 1.0 + nrm((nA, D), 0.05),
        'cm_ln_b': nrm((nA, D), 0.05),
        'cm_w_out': nrm((nA, D, D), D ** -0.5),
        'gla_w_in': nrm((nB, D, GLA_IN), D ** -0.5),
        'gla_w_gate': nrm((nB, GLA_GATE_RANK, GLA_HEADS * GLA_DK), GLA_GATE_RANK ** -0.5),
        'gla_b_gate': nrm((nB, GLA_HEADS * GLA_DK), 0.1),
        'gla_norm_g': 1.0 + nrm((nB, GLA_DV), 0.05),
        'gla_w_out': nrm((nB, GLA_HEADS * GLA_DV, D), (GLA_HEADS * GLA_DV) ** -0.5),
        'ret_w_in': nrm((nC, D, RET_IN), D ** -0.5),
        'ret_w_out': nrm((nC, RET_HEADS * RET_DV, D), (RET_HEADS * RET_DV) ** -0.5),
        'sgu_w_in': nrm((nD, D, 2 * SGU_WIDTH), D ** -0.5),
        'sgu_ln_g': 1.0 + nrm((nD, SGU_WIDTH), 0.05),
        'sgu_ln_b': nrm((nD, SGU_WIDTH), 0.05),
        'sgu_w_s': nrm((nD, SGU_HEADS, SGU_CHUNK, SGU_CHUNK), SGU_CHUNK ** -0.5),
        'sgu_b_s': 1.0 + nrm((nD, SGU_HEADS, SGU_CHUNK), 0.1),
        'sgu_w_out': nrm((nD, SGU_WIDTH, D), SGU_WIDTH ** -0.5),
    }


def reference(x, c, ada_w, ada_b, norm_g, ffn_w_in, ffn_conv_w, ffn_conv_b, ffn_w_out,
              cm_w_in, cm_dw_w, cm_dw_b, cm_ln_g, cm_ln_b, cm_w_out,
              gla_w_in, gla_w_gate, gla_b_gate, gla_norm_g, gla_w_out,
              ret_w_in, ret_w_out,
              sgu_w_in, sgu_ln_g, sgu_ln_b, sgu_w_s, sgu_b_s, sgu_w_out):
    B, L, D = x.shape
    c_act = jax.nn.silu(c)
    for i in range(DEPTH):
        m, j = i % N_MIXERS, i // N_MIXERS
        mod = (c_act @ ada_w[i] + ada_b[i]).reshape(B, N_MOD, 1, D)
        shift_m, scale_m, gate_m = mod[:, 0], mod[:, 1], mod[:, 2]
        shift_f, scale_f, gate_f = mod[:, 3], mod[:, 4], mod[:, 5]
        h = _rms_norm(x, norm_g[i, 0]) * (1.0 + scale_m) + shift_m
        if m == 0:
            y = _conv_module(h, cm_w_in[j], cm_dw_w[j], cm_dw_b[j], cm_ln_g[j], cm_ln_b[j], cm_w_out[j])
        elif m == 1:
            y = _gla_mixer(h, gla_w_in[j], gla_w_gate[j], gla_b_gate[j], gla_norm_g[j], gla_w_out[j])
        elif m == 2:
            y = _retention_mixer(h, ret_w_in[j], ret_w_out[j])
        else:
            y = _sgu_mixer(h, sgu_w_in[j], sgu_ln_g[j], sgu_ln_b[j], sgu_w_s[j], sgu_b_s[j], sgu_w_out[j])
        x = x + gate_m * _rms_norm(y, norm_g[i, 1])
        h = _rms_norm(x, norm_g[i, 2]) * (1.0 + scale_f) + shift_f
        y = _conv_ffn(h, ffn_w_in[i], ffn_conv_w[i], ffn_conv_b[i], ffn_w_out[i])
        x = x + gate_f * _rms_norm(y, norm_g[i, 3])
    return x
```

```python
import functools
import math

import jax
import jax.numpy as jnp
from jax import lax
from jax.experimental import pallas as pl
from jax.experimental.pallas import tpu as pltpu

F32 = jnp.float32
BF16 = jnp.bfloat16

EPS = 1e-6
CHUNK = 64
N_MIXERS = 4
N_MOD = 6
GLA_HEADS = 4
GLA_GATE_TEMP = 16.0
RET_HEADS = 8
ROPE_BASE = 10000.0
SGU_CHUNK = 128
SGU_HEADS = 8

LANE = 128
BF16_ROWS = 16
CONV_HALO = 32
VMEM_LIMIT = 56 * 1024 * 1024


def _cparams(n_axes):
    return pltpu.CompilerParams(
        dimension_semantics=("arbitrary",) * n_axes,
        vmem_limit_bytes=VMEM_LIMIT)


def _sigmoid(x):
    return 1.0 / (1.0 + jnp.exp(-x))


def _silu(x):
    return x * _sigmoid(x)


def _dot(a, b):
    return jnp.dot(a, b, preferred_element_type=F32)


def _dot_nt(a, b):
    return lax.dot_general(a, b, (((1,), (1,)), ((), ())), preferred_element_type=F32)


def _dot_tn(a, b):
    return lax.dot_general(a, b, (((0,), (0,)), ((), ())), preferred_element_type=F32)


def _modnorm(x, g, mod):
    ms = jnp.mean(x * x, axis=-1, keepdims=True)
    h = x * lax.rsqrt(ms + EPS) * g
    return h * (1.0 + mod[1:2, :]) + mod[0:1, :]


def _pick(n, candidates):
    for c in candidates:
        if n % c == 0:
            return c
    raise ValueError(f"no tile for {n} in {candidates}")


def _ada_kernel(c_ref, w_ref, b_ref, o_ref):
    ca = _silu(c_ref[...]).astype(BF16)
    o_ref[...] = _dot(ca, w_ref[...].astype(BF16)) + b_ref[...]


def _ada(c, ada_w, ada_b):
    depth, d, n = ada_w.shape
    bsz = c.shape[0]
    tn = _pick(n, (1536, 1024, 512, 256, 128))
    return pl.pallas_call(
        _ada_kernel,
        grid=(depth, n // tn),
        in_specs=[
            pl.BlockSpec((bsz, d), lambda l, j: (0, 0)),
            pl.BlockSpec((None, d, tn), lambda l, j: (l, 0, j)),
            pl.BlockSpec((None, 1, tn), lambda l, j: (l, 0, j)),
        ],
        out_specs=pl.BlockSpec((None, bsz, tn), lambda l, j: (l, 0, j)),
        out_shape=jax.ShapeDtypeStruct((depth, bsz, n), F32),
        compiler_params=_cparams(2),
        name="ada_proj",
    )(c, ada_w, ada_b.reshape(depth, 1, n))


def _fill_h(x_ref, g_ref, mod_ref, h_ref, off, tm, rc):
    g = g_ref[...]
    mod = mod_ref[...]
    for r in range(tm // rc):
        rows = slice(r * rc, (r + 1) * rc)
        h_ref[off + r * rc:off + (r + 1) * rc, :] = _modnorm(x_ref[rows, :], g, mod).astype(BF16)


def _mm_in_kernel(x_ref, mod_ref, g_ref, w_ref, o_ref, h_ref, *, tm, rc, act):
    @pl.when(pl.program_id(1) == 0)
    def _():
        _fill_h(x_ref, g_ref, mod_ref, h_ref, 0, tm, rc)

    for r in range(tm // rc):
        rows = slice(r * rc, (r + 1) * rc)
        y = _dot(h_ref[rows, :], w_ref[...])
        if act == "gelu":
            y = 0.5 * y * (1.0 + lax.erf(y * (1.0 / math.sqrt(2.0))))
        o_ref[rows, :] = y.astype(o_ref.dtype)


def _mm_glu_kernel(x_ref, mod_ref, g_ref, wa_ref, wb_ref, o_ref, h_ref, *, tm, rc):
    @pl.when(pl.program_id(1) == 0)
    def _():
        _fill_h(x_ref, g_ref, mod_ref, h_ref, 0, tm, rc)

    for r in range(tm // rc):
        rows = slice(r * rc, (r + 1) * rc)
        a = _dot(h_ref[rows, :], wa_ref[...])
        gate = _dot(h_ref[rows, :], wb_ref[...])
        o_ref[rows, :] = (a * _sigmoid(gate)).astype(o_ref.dtype)


def _mm_convffn_kernel(x_ref, xh_ref, mod_ref, g_ref, wa_ref, wb_ref, cw_ref, cb_ref,
                       o_ref, h_ref, *, tm, rc, tiles_per_seq):
    hr = BF16_ROWS

    @pl.when(pl.program_id(1) == 0)
    def _():
        at_start = pl.program_id(0) % tiles_per_seq == 0
        hh = _modnorm(xh_ref[...], g_ref[...], mod_ref[...])
        h_ref[0:hr, :] = jnp.where(at_start, 0.0, hh).astype(BF16)
        _fill_h(x_ref, g_ref, mod_ref, h_ref, hr, tm, rc)

    w0 = cw_ref[0:1, :]
    w1 = cw_ref[1:2, :]
    w2 = cw_ref[2:3, :]
    cb = cb_ref[...]
    tail = _dot(h_ref[0:hr, :], wa_ref[...])
    for r in range(tm // rc):
        rows = slice(hr + r * rc, hr + (r + 1) * rc)
        a = _dot(h_ref[rows, :], wa_ref[...])
        b = _dot(h_ref[rows, :], wb_ref[...])
        af = jnp.concatenate([tail, a], axis=0)
        a1 = pltpu.roll(af, 1, 0)[hr:, :]
        a2 = pltpu.roll(af, 2, 0)[hr:, :]
        conv = a2 * w0 + a1 * w1 + a * w2 + cb
        o_ref[r * rc:(r + 1) * rc, :] = (_silu(conv) * b).astype(o_ref.dtype)
        tail = a[rc - hr:, :]


def _mm_in(x, mod5, layer, slot, g, w, *, seq, kind, out_dtype, conv_w=None, conv_b=None,
           tm=None, tn=None, rc=256):
    t, d = x.shape
    n = w.shape[1]
    paired = kind in ("glu", "convffn")
    n_out = n // 2 if paired else n
    tm = tm or _pick(seq, (1024, 512, 256, 128))
    tn = tn or _pick(n_out, (512, 256, 128))
    rc = min(rc, tm)
    tps = seq // tm
    nb = n_out // tn
    grid = (t // tm, nb)
    x_spec = pl.BlockSpec((tm, d), lambda i, j: (i, 0))
    mod_spec = pl.BlockSpec((None, None, None, 3, d), lambda i, j: (layer, i // tps, slot, 0, 0))
    g_spec = pl.BlockSpec((1, d), lambda i, j: (0, 0))
    wa_spec = pl.BlockSpec((d, tn), lambda i, j: (0, j))
    wb_spec = pl.BlockSpec((d, tn), lambda i, j: (0, j + nb))
    out_spec = pl.BlockSpec((tm, tn), lambda i, j: (i, j))
    out_shape = jax.ShapeDtypeStruct((t, n_out), out_dtype)
    g2 = g.reshape(1, d)
    if kind in ("plain", "gelu"):
        return pl.pallas_call(
            functools.partial(_mm_in_kernel, tm=tm, rc=rc, act=kind),
            grid=grid,
            in_specs=[x_spec, mod_spec, g_spec, wa_spec],
            out_specs=out_spec, out_shape=out_shape,
            scratch_shapes=[pltpu.VMEM((tm, d), BF16)],
            compiler_params=_cparams(2), name=f"mm_in_{kind}",
        )(x, mod5, g2, w)
    if kind == "glu":
        return pl.pallas_call(
            functools.partial(_mm_glu_kernel, tm=tm, rc=rc),
            grid=grid,
            in_specs=[x_spec, mod_spec, g_spec, wa_spec, wb_spec],
            out_specs=out_spec, out_shape=out_shape,
            scratch_shapes=[pltpu.VMEM((tm, d), BF16)],
            compiler_params=_cparams(2), name="mm_in_glu",
        )(x, mod5, g2, w, w)
    assert kind == "convffn"
    hb = tm // BF16_ROWS
    xh_spec = pl.BlockSpec((BF16_ROWS, d), lambda i, j: (jnp.maximum(i * hb - 1, 0), 0))
    kw = conv_w.shape[0]
    return pl.pallas_call(
        functools.partial(_mm_convffn_kernel, tm=tm, rc=rc, tiles_per_seq=tps),
        grid=grid,
        in_specs=[x_spec, xh_spec, mod_spec, g_spec, wa_spec, wb_spec,
                  pl.BlockSpec((kw, tn), lambda i, j: (0, j)),
                  pl.BlockSpec((1, tn), lambda i, j: (0, j))],
        out_specs=out_spec, out_shape=out_shape,
        scratch_shapes=[pltpu.VMEM((tm + BF16_ROWS, d), BF16)],
        compiler_params=_cparams(2), name="mm_in_convffn",
    )(x, x, mod5, g2, w, w, conv_w, conv_b.reshape(1, n_out))


def _post(y, x, g, mod):
    ms = jnp.mean(y * y, axis=-1, keepdims=True)
    return x + mod[2:3, :] * (y * lax.rsqrt(ms + EPS) * g)


def _mm_out_kernel(u_ref, w_ref, x_ref, mod_ref, g_ref, o_ref, *, tm, rc):
    g = g_ref[...]
    mod = mod_ref[...]
    for r in range(tm // rc):
        rows = slice(r * rc, (r + 1) * rc)
        y = _dot(u_ref[rows, :], w_ref[...])
        o_ref[rows, :] = _post(y, x_ref[rows, :], g, mod)


def _mm_out(u, w, x, mod5, layer, slot, g, *, seq, tm=None, rc=256):
    t, k = u.shape
    d = w.shape[1]
    tm = tm or _pick(seq, (512, 256, 128))
    rc = min(rc, tm)
    tps = seq // tm
    return pl.pallas_call(
        functools.partial(_mm_out_kernel, tm=tm, rc=rc),
        grid=(t // tm,),
        in_specs=[
            pl.BlockSpec((tm, k), lambda i: (i, 0)),
            pl.BlockSpec((k, d), lambda i: (0, 0), pipeline_mode=pl.Buffered(1)),
            pl.BlockSpec((tm, d), lambda i: (i, 0)),
            pl.BlockSpec((None, None, None, 3, d), lambda i: (layer, i // tps, slot, 0, 0)),
            pl.BlockSpec((1, d), lambda i: (0, 0)),
        ],
        out_specs=pl.BlockSpec((tm, d), lambda i: (i, 0)),
        out_shape=jax.ShapeDtypeStruct((t, d), F32),
        compiler_params=_cparams(1), name="mm_out",
    )(u, w, x, mod5, g.reshape(1, d))


def _conv_out_kernel(z_ref, zh_ref, dw_ref, db_ref, lg_ref, lb_ref, w_ref, x_ref, mod_ref, g_ref,
                     o_ref, zf_ref, hb_ref, *, tm, kw, rc, cc, tiles_per_seq):
    d = z_ref.shape[1]
    halo = CONV_HALO
    at_start = pl.program_id(0) % tiles_per_seq == 0
    zf_ref[0:halo, :] = jnp.where(at_start, 0.0, zh_ref[...])
    zf_ref[halo:, :] = z_ref[...]
    base = halo - (kw - 1)
    lg = lg_ref[...]
    lb = lb_ref[...]
    for r in range(tm // rc):
        parts = []
        for c in range(d // cc):
            cols = slice(c * cc, (c + 1) * cc)
            acc = jnp.broadcast_to(db_ref[:, cols], (rc, cc))
            for k in range(kw):
                acc = acc + zf_ref[base + r * rc + k:base + r * rc + k + rc, cols] * dw_ref[k:k + 1, cols]
            parts.append(acc)
        zc = jnp.concatenate(parts, axis=1)
        mu = jnp.mean(zc, axis=-1, keepdims=True)
        zc = zc - mu
        var = jnp.mean(zc * zc, axis=-1, keepdims=True)
        y = zc * lax.rsqrt(var + EPS) * lg + lb
        hb_ref[r * rc:(r + 1) * rc, :] = _silu(y).astype(BF16)
    g = g_ref[...]
    mod = mod_ref[...]
    ro = 128
    for r in range(tm // ro):
        rows = slice(r * ro, (r + 1) * ro)
        y = _dot(hb_ref[rows, :], w_ref[...])
        o_ref[rows, :] = _post(y, x_ref[rows, :], g, mod)


def _conv_out(z, dw_w, dw_b, ln_g, ln_b, w, x, mod5, layer, slot, g, *, seq, tm=None):
    t, d = z.shape
    kw = dw_w.shape[0]
    assert kw - 1 <= CONV_HALO
    tm = tm or _pick(seq, (256, 128))
    tps = seq // tm
    hb = tm // CONV_HALO
    full = lambda i: (0, 0)
    return pl.pallas_call(
        functools.partial(_conv_out_kernel, tm=tm, kw=kw, rc=32, cc=256, tiles_per_seq=tps),
        grid=(t // tm,),
        in_specs=[
            pl.BlockSpec((tm, d), lambda i: (i, 0)),
            pl.BlockSpec((CONV_HALO, d), lambda i: (jnp.maximum(i * hb - 1, 0), 0)),
            pl.BlockSpec((kw, d), full),
            pl.BlockSpec((1, d), full),
            pl.BlockSpec((1, d), full),
            pl.BlockSpec((1, d), full),
            pl.BlockSpec((d, d), full, pipeline_mode=pl.Buffered(1)),
            pl.BlockSpec((tm, d), lambda i: (i, 0)),
            pl.BlockSpec((None, None, None, 3, d), lambda i: (layer, i // tps, slot, 0, 0)),
            pl.BlockSpec((1, d), full),
        ],
        out_specs=pl.BlockSpec((tm, d), lambda i: (i, 0)),
        out_shape=jax.ShapeDtypeStruct((t, d), F32),
        scratch_shapes=[pltpu.VMEM((tm + CONV_HALO, d), F32), pltpu.VMEM((tm, d), BF16)],
        compiler_params=_cparams(1), name="conv_out",
    )(z, z, dw_w, dw_b.reshape(1, d), ln_g.reshape(1, d), ln_b.reshape(1, d), w, x, mod5,
      g.reshape(1, d))


def _split3(a):
    hi = a.astype(BF16)
    r1 = a - hi.astype(F32)
    mid = r1.astype(BF16)
    lo = (r1 - mid.astype(F32)).astype(BF16)
    return hi, mid, lo


def _lane_col(row, n_rep):
    n = row.shape[1]
    col = jnp.transpose(jnp.broadcast_to(row, (LANE, n)))
    return jnp.concatenate([col] * n_rep, axis=1)


def _gla_kernel(q_ref, k_ref, v_ref, r_ref, gl_ref, wg_ref, bg_ref, ng_ref, o_ref, st_ref,
                *, heads, dk, dv, tm):
    @pl.when(pl.program_id(1) == 0)
    def _():
        st_ref[...] = jnp.zeros_like(st_ref)

    ti = lax.broadcasted_iota(jnp.int32, (CHUNK, CHUNK), 0)
    si = lax.broadcasted_iota(jnp.int32, (CHUNK, CHUNK), 1)
    causal = si <= ti
    tril = jnp.where(causal, 1.0, 0.0).astype(BF16)
    scale = dk ** -0.5
    ng = ng_ref[...]
    for c in range(tm // CHUNK):
        rows = slice(c * CHUNK, (c + 1) * CHUNK)
        zg = _dot(gl_ref[rows, :].astype(BF16), wg_ref[...]) + bg_ref[...]
        log_a = (jnp.minimum(zg, 0.0) - jnp.log1p(jnp.exp(-jnp.abs(zg)))) * (1.0 / GLA_GATE_TEMP)
        for h in range(heads):
            kc = slice(h * dk, (h + 1) * dk)
            vc = slice(h * dv, (h + 1) * dv)
            hi, mid, lo = _split3(log_a[:, kc])
            b = _dot(tril, hi) + _dot(tril, mid) + _dot(tril, lo)
            b_last = b[CHUNK - 1:CHUNK, :]
            e_pos = jnp.exp(b)
            e_neg = jnp.exp(-b)
            q = q_ref[rows, kc] * scale
            k = k_ref[rows, kc]
            v = v_ref[rows, vc].astype(BF16)
            q_fwd = (q * e_pos).astype(BF16)
            k_fwd = (k * e_neg).astype(BF16)
            q_bwd = (q * e_neg).astype(BF16)
            k_bwd = (k * e_pos).astype(BF16)
            scores = jnp.where(causal, _dot_nt(q_fwd, k_fwd), _dot_nt(q_bwd, k_bwd))
            o = _dot(scores.astype(BF16), v)
            k_state = (k * jnp.exp(b_last - b)).astype(BF16)
            st = st_ref[h]
            o = o + _dot(q_fwd, st.astype(BF16))
            st_ref[h] = st * _lane_col(jnp.exp(b_last), dv // LANE) + _dot_tn(k_state, v)
            ms = jnp.mean(o * o, axis=-1, keepdims=True)
            o = o * lax.rsqrt(ms + EPS) * ng
            o_ref[rows, vc] = (o * _silu(r_ref[rows, vc])).astype(o_ref.dtype)


def _gla_core(qkvr, glow, w_gate, b_gate, norm_g, *, bsz, seq, tm=None):
    heads = GLA_HEADS
    hk = w_gate.shape[1]
    dk = hk // heads
    dv = norm_g.shape[0]
    hv = heads * dv
    assert qkvr.shape[1] == 2 * hk + 2 * hv and hv % hk == 0
    tm = tm or _pick(seq, (256, 128, 64))
    tps = seq // tm
    row = lambda b, t: b * tps + t
    return pl.pallas_call(
        functools.partial(_gla_kernel, heads=heads, dk=dk, dv=dv, tm=tm),
        grid=(bsz, tps),
        in_specs=[
            pl.BlockSpec((tm, hk), lambda b, t: (row(b, t), 0)),
            pl.BlockSpec((tm, hk), lambda b, t: (row(b, t), 1)),
            pl.BlockSpec((tm, hv), lambda b, t: (row(b, t), (2 * hk) // hv)),
            pl.BlockSpec((tm, hv), lambda b, t: (row(b, t), (2 * hk) // hv + 1)),
            pl.BlockSpec((tm, LANE), lambda b, t: (row(b, t), 0)),
            pl.BlockSpec((LANE, hk), lambda b, t: (0, 0)),
            pl.BlockSpec((1, hk), lambda b, t: (0, 0)),
            pl.BlockSpec((1, dv), lambda b, t: (0, 0)),
        ],
        out_specs=pl.BlockSpec((tm, hv), lambda b, t: (row(b, t), 0)),
        out_shape=jax.ShapeDtypeStruct((bsz * seq, hv), BF16),
        scratch_shapes=[pltpu.VMEM((heads, dk, dv), F32)],
        compiler_params=_cparams(2), name="gla_core",
    )(qkvr, qkvr, qkvr, qkvr, glow, w_gate, b_gate.reshape(1, hk), norm_g.reshape(1, dv))


def _ret_kernel(q_ref, k_ref, v_ref, g_ref, cos_ref, sin_ref, o_ref, st_ref, *, heads, dk, dv, tm):
    @pl.when(pl.program_id(1) == 0)
    def _():
        st_ref[...] = jnp.zeros_like(st_ref)

    half = dk // 2
    ti = lax.broadcasted_iota(jnp.int32, (CHUNK, CHUNK), 0).astype(F32)
    si = lax.broadcasted_iota(jnp.int32, (CHUNK, CHUNK), 1).astype(F32)
    dist = jnp.abs(ti - si)
    idx = lax.broadcasted_iota(jnp.int32, (CHUNK, dk), 0).astype(F32)
    kscale = dk ** -0.5

    def rot(t, cos, sin):
        t1 = t[:, :half]
        t2 = t[:, half:]
        return jnp.concatenate([t1 * cos - t2 * sin, t1 * sin + t2 * cos], axis=1)

    for h in range(heads):
        log_g = math.log1p(-(2.0 ** (-5.0 - h)))
        d_intra = jnp.exp(log_g * dist)
        q_dec = jnp.exp(log_g * (idx + 1.0))
        k_dec = jnp.exp(log_g * (CHUNK - 1.0 - idx))
        chunk_decay = math.exp(log_g * CHUNK)
        kc = slice(h * dk, (h + 1) * dk)
        vc = slice(h * dv, (h + 1) * dv)
        for c in range(tm // CHUNK):
            rows = slice(c * CHUNK, (c + 1) * CHUNK)
            cos = cos_ref[rows, :]
            sin = sin_ref[rows, :]
            q = rot(q_ref[rows, kc], cos, sin)
            k = rot(k_ref[rows, kc], cos, sin) * kscale
            v = v_ref[rows, vc].astype(BF16)
            scores = _dot_nt(q.astype(BF16), k.astype(BF16)) * d_intra
            o = _dot(scores.astype(BF16), v)
            st = st_ref[h]
            o = o + _dot((q * q_dec).astype(BF16), st.astype(BF16))
            st_ref[h] = st * chunk_decay + _dot_tn((k * k_dec).astype(BF16), v)
            ms = jnp.mean(o * o, axis=-1, keepdims=True)
            o = o * lax.rsqrt(ms + EPS)
            o_ref[rows, vc] = (o * _silu(g_ref[rows, vc])).astype(o_ref.dtype)


def _ret_core(qkvg, *, bsz, seq, tm=None):
    heads = RET_HEADS
    n = qkvg.shape[1]
    hk = n // 6
    hv = 2 * hk
    dk = hk // heads
    dv = hv // heads
    half = dk // 2
    tm = tm or _pick(seq, (256, 128, 64))
    tps = seq // tm
    pos = jnp.arange(seq)
    inv_freq = ROPE_BASE ** (-jnp.arange(half, dtype=F32) / half)
    ang = pos.astype(F32)[:, None] * inv_freq
    cos = jnp.cos(ang)
    sin = jnp.sin(ang)
    row = lambda b, t: b * tps + t
    return pl.pallas_call(
        functools.partial(_ret_kernel, heads=heads, dk=dk, dv=dv, tm=tm),
        grid=(bsz, tps),
        in_specs=[
            pl.BlockSpec((tm, hk), lambda b, t: (row(b, t), 0)),
            pl.BlockSpec((tm, hk), lambda b, t: (row(b, t), 1)),
            pl.BlockSpec((tm, hv), lambda b, t: (row(b, t), 1)),
            pl.BlockSpec((tm, hv), lambda b, t: (row(b, t), 2)),
            pl.BlockSpec((tm, half), lambda b, t: (t, 0)),
            pl.BlockSpec((tm, half), lambda b, t: (t, 0)),
        ],
        out_specs=pl.BlockSpec((tm, hv), lambda b, t: (row(b, t), 0)),
        out_shape=jax.ShapeDtypeStruct((bsz * seq, hv), BF16),
        scratch_shapes=[pltpu.VMEM((heads, dk, dv), F32)],
        compiler_params=_cparams(2), name="ret_core",
    )(qkvg, qkvg, qkvg, qkvg, cos, sin)


def _sgu_kernel(u_ref, v_ref, lg_ref, lb_ref, ws_ref, bs_ref, o_ref, vn_ref, *, heads, hd, tm):
    lg = lg_ref[...]
    lb = lb_ref[...]
    rc = 64
    for r in range(tm // rc):
        rows = slice(r * rc, (r + 1) * rc)
        v = v_ref[rows, :]
        mu = jnp.mean(v, axis=-1, keepdims=True)
        v = v - mu
        var = jnp.mean(v * v, axis=-1, keepdims=True)
        vn_ref[rows, :] = (v * lax.rsqrt(var + EPS) * lg + lb).astype(BF16)
    pi = lax.broadcasted_iota(jnp.int32, (SGU_CHUNK, SGU_CHUNK), 0)
    pj = lax.broadcasted_iota(jnp.int32, (SGU_CHUNK, SGU_CHUNK), 1)
    mask = (pj // CHUNK) <= (pi // CHUNK)
    for h in range(heads):
        w = jnp.where(mask, ws_ref[h], 0.0).astype(BF16)
        bias = bs_ref[h]
        cols = slice(h * hd, (h + 1) * hd)
        for n in range(tm // SGU_CHUNK):
            rows = slice(n * SGU_CHUNK, (n + 1) * SGU_CHUNK)
            sv = _dot(w, vn_ref[rows, cols]) + bias
            o_ref[rows, cols] = (u_ref[rows, cols] * sv).astype(o_ref.dtype)


def _sgu_core(uv, ln_g, ln_b, w_s, b_s, *, seq, tm=None):
    t = uv.shape[0]
    width = uv.shape[1] // 2
    heads = SGU_HEADS
    hd = width // heads
    tm = tm or _pick(seq, (256, 128))
    full2 = lambda i: (0, 0)
    full3 = lambda i: (0, 0, 0)
    return pl.pallas_call(
        functools.partial(_sgu_kernel, heads=heads, hd=hd, tm=tm),
        grid=(t // tm,),
        in_specs=[
            pl.BlockSpec((tm, width), lambda i: (i, 0)),
            pl.BlockSpec((tm, width), lambda i: (i, 1)),
            pl.BlockSpec((1, width), full2),
            pl.BlockSpec((1, width), full2),
            pl.BlockSpec((heads, SGU_CHUNK, SGU_CHUNK), full3),
            pl.BlockSpec((heads, SGU_CHUNK, 1), full3),
        ],
        out_specs=pl.BlockSpec((tm, width), lambda i: (i, 0)),
        out_shape=jax.ShapeDtypeStruct((t, width), BF16),
        scratch_shapes=[pltpu.VMEM((tm, width), BF16)],
        compiler_params=_cparams(1), name="sgu_core",
    )(uv, uv, ln_g.reshape(1, width), ln_b.reshape(1, width), w_s,
      b_s.reshape(heads, SGU_CHUNK, 1))


def kernel(x, c, ada_w, ada_b, norm_g, ffn_w_in, ffn_conv_w, ffn_conv_b, ffn_w_out, cm_w_in, cm_dw_w, cm_dw_b, cm_ln_g, cm_ln_b, cm_w_out, gla_w_in, gla_w_gate, gla_b_gate, gla_norm_g, gla_w_out, ret_w_in, ret_w_out, sgu_w_in, sgu_ln_g, sgu_ln_b, sgu_w_s, sgu_b_s, sgu_w_out):
    bsz, seq, d = x.shape
    depth = ada_w.shape[0]
    mod5 = _ada(c, ada_w, ada_b).reshape(depth, bsz, 2, 3, d)
    xf = x.reshape(bsz * seq, d)
    bf = lambda a: a.astype(BF16)
    for i in range(depth):
        m, j = i % N_MIXERS, i // N_MIXERS
        mm_in = functools.partial(_mm_in, xf, mod5, i, 0, norm_g[i, 0], seq=seq)
        post = dict(x=xf, mod5=mod5, layer=i, slot=0, g=norm_g[i, 1], seq=seq)
        if m == 0:
            z = mm_in(bf(cm_w_in[j]), kind="glu", out_dtype=F32)
            xf = _conv_out(z, cm_dw_w[j], cm_dw_b[j], cm_ln_g[j], cm_ln_b[j], bf(cm_w_out[j]), **post)
        elif m == 1:
            hk = gla_w_gate.shape[2]
            n_main = gla_w_in.shape[2] - gla_w_gate.shape[1]
            rank = gla_w_gate.shape[1]
            w_low = jnp.pad(gla_w_in[j][:, n_main:], ((0, 0), (0, LANE - rank)))
            w_gate = jnp.pad(gla_w_gate[j], ((0, LANE - rank), (0, 0)))
            qkvr = mm_in(bf(gla_w_in[j][:, :n_main]), kind="plain", out_dtype=F32)
            glow = mm_in(bf(w_low), kind="plain", out_dtype=F32)
            o = _gla_core(qkvr, glow, bf(w_gate), gla_b_gate[j], gla_norm_g[j], bsz=bsz, seq=seq)
            xf = _mm_out(o, bf(gla_w_out[j]), **post)
        elif m == 2:
            qkvg = mm_in(bf(ret_w_in[j]), kind="plain", out_dtype=F32)
            o = _ret_core(qkvg, bsz=bsz, seq=seq)
            xf = _mm_out(o, bf(ret_w_out[j]), **post)
        else:
            uv = mm_in(bf(sgu_w_in[j]), kind="gelu", out_dtype=F32)
            o = _sgu_core(uv, sgu_ln_g[j], sgu_ln_b[j], sgu_w_s[j], sgu_b_s[j], seq=seq)
            xf = _mm_out(o, bf(sgu_w_out[j]), **post)
        u = _mm_in(xf, mod5, i, 1, norm_g[i, 2], bf(ffn_w_in[i]), seq=seq, kind="convffn",
                   out_dtype=BF16, conv_w=ffn_conv_w[i], conv_b=ffn_conv_b[i])
        xf = _mm_out(u, bf(ffn_w_out[i]), xf, mod5, i, 1, norm_g[i, 3], seq=seq)
    return xf.reshape(bsz, seq, d)
```

```python
import functools
import math

import jax
import jax.numpy as jnp
from jax import lax
from jax.experimental import pallas as pl
from jax.experimental.pallas import tpu as pltpu

F32 = jnp.float32
BF16 = jnp.bfloat16

EPS = 1e-6
CHUNK = 64
N_MIXERS = 4
N_MOD = 6
GLA_HEADS = 4
GLA_GATE_TEMP = 16.0
RET_HEADS = 8
ROPE_BASE = 10000.0
SGU_CHUNK = 128
SGU_HEADS = 8

LANE = 128
BF16_ROWS = 16
CONV_HALO = 32
VMEM_LIMIT = 56 * 1024 * 1024


def _cparams(n_axes):
    return pltpu.CompilerParams(
        dimension_semantics=("arbitrary",) * n_axes,
        vmem_limit_bytes=VMEM_LIMIT)


def _sigmoid(x):
    return 1.0 / (1.0 + jnp.exp(-x))


def _silu(x):
    return x * _sigmoid(x)


def _dot(a, b):
    return jnp.dot(a, b, preferred_element_type=F32)


def _dot_nt(a, b):
    return lax.dot_general(a, b, (((1,), (1,)), ((), ())), preferred_element_type=F32)


def _dot_tn(a, b):
    return lax.dot_general(a, b, (((0,), (0,)), ((), ())), preferred_element_type=F32)


def _modnorm(x, g, mod):
    ms = jnp.mean(x * x, axis=-1, keepdims=True)
    h = x * lax.rsqrt(ms + EPS) * g
    return h * (1.0 + mod[1:2, :]) + mod[0:1, :]


def _pick(n, candidates):
    for c in candidates:
        if n % c == 0:
            return c
    raise ValueError(f"no tile for {n} in {candidates}")


def _ada_kernel(c_ref, w_ref, b_ref, o_ref):
    ca = _silu(c_ref[...]).astype(BF16)
    o_ref[...] = _dot(ca, w_ref[...].astype(BF16)) + b_ref[...]


def _ada(c, ada_w, ada_b):
    depth, d, n = ada_w.shape
    bsz = c.shape[0]
    tn = _pick(n, (1536, 1024, 512, 256, 128))
    return pl.pallas_call(
        _ada_kernel,
        grid=(depth, n // tn),
        in_specs=[
            pl.BlockSpec((bsz, d), lambda l, j: (0, 0)),
            pl.BlockSpec((None, d, tn), lambda l, j: (l, 0, j)),
            pl.BlockSpec((None, 1, tn), lambda l, j: (l, 0, j)),
        ],
        out_specs=pl.BlockSpec((None, bsz, tn), lambda l, j: (l, 0, j)),
        out_shape=jax.ShapeDtypeStruct((depth, bsz, n), F32),
        compiler_params=_cparams(2),
        name="ada_proj",
    )(c, ada_w, ada_b.reshape(depth, 1, n))


def _fill_h(x_ref, g_ref, mod_ref, h_ref, off, tm, rc):
    g = g_ref[...]
    mod = mod_ref[...]
    for r in range(tm // rc):
        rows = slice(r * rc, (r + 1) * rc)
        h_ref[off + r * rc:off + (r + 1) * rc, :] = _modnorm(x_ref[rows, :], g, mod).astype(BF16)


def _mm_in_kernel(x_ref, mod_ref, g_ref, w_ref, o_ref, h_ref, *, tm, rc, act):
    @pl.when(pl.program_id(1) == 0)
    def _():
        _fill_h(x_ref, g_ref, mod_ref, h_ref, 0, tm, rc)

    for r in range(tm // rc):
        rows = slice(r * rc, (r + 1) * rc)
        y = _dot(h_ref[rows, :], w_ref[...])
        if act == "gelu":
            y = 0.5 * y * (1.0 + lax.erf(y * (1.0 / math.sqrt(2.0))))
        o_ref[rows, :] = y.astype(o_ref.dtype)


def _mm_glu_kernel(x_ref, mod_ref, g_ref, wa_ref, wb_ref, o_ref, h_ref, *, tm, rc):
    @pl.when(pl.program_id(1) == 0)
    def _():
        _fill_h(x_ref, g_ref, mod_ref, h_ref, 0, tm, rc)

    for r in range(tm // rc):
        rows = slice(r * rc, (r + 1) * rc)
        a = _dot(h_ref[rows, :], wa_ref[...])
        gate = _dot(h_ref[rows, :], wb_ref[...])
        o_ref[rows, :] = (a * _sigmoid(gate)).astype(o_ref.dtype)


def _mm_convffn_kernel(x_ref, mod_ref, g_ref, wa_ref, wb_ref, cw_ref, cb_ref,
                       o_ref, h_ref, tail_ref, *, tm, rc, tiles_per_seq):
    hr = 8
    j = pl.program_id(1)

    @pl.when(j == 0)
    def _():
        _fill_h(x_ref, g_ref, mod_ref, h_ref, 0, tm, rc)

    @pl.when(pl.program_id(0) % tiles_per_seq == 0)
    def _():
        tail_ref[j] = jnp.zeros(tail_ref.shape[1:], F32)

    w0 = cw_ref[0:1, :]
    w1 = cw_ref[1:2, :]
    w2 = cw_ref[2:3, :]
    cb = cb_ref[...]
    tail = tail_ref[j]
    for r in range(tm // rc):
        rows = slice(r * rc, (r + 1) * rc)
        a = _dot(h_ref[rows, :], wa_ref[...])
        b = _dot(h_ref[rows, :], wb_ref[...])
        af = jnp.concatenate([tail, a], axis=0)
        a1 = pltpu.roll(af, 1, 0)[hr:, :]
        a2 = pltpu.roll(af, 2, 0)[hr:, :]
        conv = a2 * w0 + a1 * w1 + a * w2 + cb
        o_ref[rows, :] = (_silu(conv) * b).astype(o_ref.dtype)
        tail = a[rc - hr:, :]
    tail_ref[j] = tail


def _mm_in(x, mod5, layer, slot, g, w, *, seq, kind, out_dtype, conv_w=None, conv_b=None,
           tm=None, tn=None, rc=256):
    t, d = x.shape
    n = w.shape[1]
    paired = kind in ("glu", "convffn")
    n_out = n // 2 if paired else n
    tm = tm or _pick(seq, (1024, 512, 256, 128))
    tn = tn or _pick(n_out, (1024, 512, 256, 128))
    rc = min(rc, tm)
    tps = seq // tm
    nb = n_out // tn
    grid = (t // tm, nb)
    x_spec = pl.BlockSpec((tm, d), lambda i, j: (i, 0))
    mod_spec = pl.BlockSpec((None, None, None, 3, d), lambda i, j: (layer, i // tps, slot, 0, 0))
    g_spec = pl.BlockSpec((1, d), lambda i, j: (0, 0))
    wa_spec = pl.BlockSpec((d, tn), lambda i, j: (0, j))
    wb_spec = pl.BlockSpec((d, tn), lambda i, j: (0, j + nb))
    out_spec = pl.BlockSpec((tm, tn), lambda i, j: (i, j))
    out_shape = jax.ShapeDtypeStruct((t, n_out), out_dtype)
    g2 = g.reshape(1, d)
    if kind in ("plain", "gelu"):
        return pl.pallas_call(
            functools.partial(_mm_in_kernel, tm=tm, rc=rc, act=kind),
            grid=grid,
            in_specs=[x_spec, mod_spec, g_spec, wa_spec],
            out_specs=out_spec, out_shape=out_shape,
            scratch_shapes=[pltpu.VMEM((tm, d), BF16)],
            compiler_params=_cparams(2), name=f"mm_in_{kind}",
        )(x, mod5, g2, w)
    if kind == "glu":
        return pl.pallas_call(
            functools.partial(_mm_glu_kernel, tm=tm, rc=rc),
            grid=grid,
            in_specs=[x_spec, mod_spec, g_spec, wa_spec, wb_spec],
            out_specs=out_spec, out_shape=out_shape,
            scratch_shapes=[pltpu.VMEM((tm, d), BF16)],
            compiler_params=_cparams(2), name="mm_in_glu",
        )(x, mod5, g2, w, w)
    assert kind == "convffn"
    kw = conv_w.shape[0]
    assert kw == 3
    return pl.pallas_call(
        functools.partial(_mm_convffn_kernel, tm=tm, rc=rc, tiles_per_seq=tps),
        grid=grid,
        in_specs=[x_spec, mod_spec, g_spec, wa_spec, wb_spec,
                  pl.BlockSpec((kw, tn), lambda i, j: (0, j)),
                  pl.BlockSpec((1, tn), lambda i, j: (0, j))],
        out_specs=out_spec, out_shape=out_shape,
        scratch_shapes=[pltpu.VMEM((tm, d), BF16), pltpu.VMEM((nb, 8, tn), F32)],
        compiler_params=_cparams(2), name="mm_in_convffn",
    )(x, mod5, g2, w, w, conv_w, conv_b.reshape(1, n_out))


def _post(y, x, g, mod):
    ms = jnp.mean(y * y, axis=-1, keepdims=True)
    return x + mod[2:3, :] * (y * lax.rsqrt(ms + EPS) * g)


def _mm_out_kernel(u_ref, w_ref, x_ref, mod_ref, g_ref, o_ref, *, tm, rc):
    g = g_ref[...]
    mod = mod_ref[...]
    for r in range(tm // rc):
        rows = slice(r * rc, (r + 1) * rc)
        y = _dot(u_ref[rows, :], w_ref[...])
        o_ref[rows, :] = _post(y, x_ref[rows, :], g, mod)


def _mm_out(u, w, x, mod5, layer, slot, g, *, seq, tm=None, rc=256):
    t, k = u.shape
    d = w.shape[1]
    tm = tm or _pick(seq, (512, 256, 128))
    rc = min(rc, tm)
    tps = seq // tm
    return pl.pallas_call(
        functools.partial(_mm_out_kernel, tm=tm, rc=rc),
        grid=(t // tm,),
        in_specs=[
            pl.BlockSpec((tm, k), lambda i: (i, 0)),
            pl.BlockSpec((k, d), lambda i: (0, 0), pipeline_mode=pl.Buffered(1)),
            pl.BlockSpec((tm, d), lambda i: (i, 0)),
            pl.BlockSpec((None, None, None, 3, d), lambda i: (layer, i // tps, slot, 0, 0)),
            pl.BlockSpec((1, d), lambda i: (0, 0)),
        ],
        out_specs=pl.BlockSpec((tm, d), lambda i: (i, 0)),
        out_shape=jax.ShapeDtypeStruct((t, d), F32),
        compiler_params=_cparams(1), name="mm_out",
    )(u, w, x, mod5, g.reshape(1, d))


def _conv_out_kernel(z_ref, zh_ref, dw_ref, db_ref, lg_ref, lb_ref, w_ref, x_ref, mod_ref, g_ref,
                     o_ref, zf_ref, zc_ref, hb_ref, *, tm, kw, rc, cc, rl, tiles_per_seq):
    d = z_ref.shape[1]
    halo = CONV_HALO
    sub = 8
    at_start = pl.program_id(0) % tiles_per_seq == 0
    zf_ref[0:halo, :] = jnp.where(at_start, 0.0, zh_ref[...])
    zf_ref[halo:, :] = z_ref[...]
    base = halo - (kw - 1)
    n = rc + halo
    for r in range(tm // rc):
        for c in range(d // cc):
            cols = slice(c * cc, (c + 1) * cc)
            blk = zf_ref[r * rc:r * rc + n, cols]
            acc = jnp.broadcast_to(db_ref[:, cols], (rc, cc))
            for s in range(sub):
                sh = blk if s == 0 else pltpu.roll(blk, n - s, 0)
                for q in range(halo // sub + 1):
                    k = sub * q + s - base
                    if 0 <= k < kw:
                        acc = acc + sh[sub * q:sub * q + rc, :] * dw_ref[k:k + 1, cols]
            zc_ref[r * rc:(r + 1) * rc, cols] = acc
    lg = lg_ref[...]
    lb = lb_ref[...]
    for r in range(tm // rl):
        rows = slice(r * rl, (r + 1) * rl)
        zc = zc_ref[rows, :]
        mu = jnp.mean(zc, axis=-1, keepdims=True)
        zc = zc - mu
        var = jnp.mean(zc * zc, axis=-1, keepdims=True)
        y = zc * lax.rsqrt(var + EPS) * lg + lb
        hb_ref[rows, :] = _silu(y).astype(BF16)
    g = g_ref[...]
    mod = mod_ref[...]
    ro = 128
    for r in range(tm // ro):
        rows = slice(r * ro, (r + 1) * ro)
        y = _dot(hb_ref[rows, :], w_ref[...])
        o_ref[rows, :] = _post(y, x_ref[rows, :], g, mod)


def _conv_out(z, dw_w, dw_b, ln_g, ln_b, w, x, mod5, layer, slot, g, *, seq, tm=None):
    t, d = z.shape
    kw = dw_w.shape[0]
    assert kw - 1 <= CONV_HALO
    tm = tm or _pick(seq, (256, 128))
    tps = seq // tm
    hb = tm // CONV_HALO
    full = lambda i: (0, 0)
    return pl.pallas_call(
        functools.partial(_conv_out_kernel, tm=tm, kw=kw, rc=min(128, tm), cc=LANE, rl=32,
                          tiles_per_seq=tps),
        grid=(t // tm,),
        in_specs=[
            pl.BlockSpec((tm, d), lambda i: (i, 0)),
            pl.BlockSpec((CONV_HALO, d), lambda i: (jnp.maximum(i * hb - 1, 0), 0)),
            pl.BlockSpec((kw, d), full),
            pl.BlockSpec((1, d), full),
            pl.BlockSpec((1, d), full),
            pl.BlockSpec((1, d), full),
            pl.BlockSpec((d, d), full, pipeline_mode=pl.Buffered(1)),
            pl.BlockSpec((tm, d), lambda i: (i, 0)),
            pl.BlockSpec((None, None, None, 3, d), lambda i: (layer, i // tps, slot, 0, 0)),
            pl.BlockSpec((1, d), full),
        ],
        out_specs=pl.BlockSpec((tm, d), lambda i: (i, 0)),
        out_shape=jax.ShapeDtypeStruct((t, d), F32),
        scratch_shapes=[pltpu.VMEM((tm + CONV_HALO, d), F32), pltpu.VMEM((tm, d), F32),
                        pltpu.VMEM((tm, d), BF16)],
        compiler_params=_cparams(1), name="conv_out",
    )(z, z, dw_w, dw_b.reshape(1, d), ln_g.reshape(1, d), ln_b.reshape(1, d), w, x, mod5,
      g.reshape(1, d))


def _split3(a):
    hi = a.astype(BF16)
    r1 = a - hi.astype(F32)
    mid = r1.astype(BF16)
    lo = (r1 - mid.astype(F32)).astype(BF16)
    return hi, mid, lo


def _gla_kernel(q_ref, k_ref, v_ref, r_ref, gl_ref, wg_ref, bg_ref, ng_ref, o_ref, st_ref,
                *, heads, dk, dv, tm):
    @pl.when(pl.program_id(1) == 0)
    def _():
        st_ref[...] = jnp.zeros_like(st_ref)

    hk = heads * dk
    ti = lax.broadcasted_iota(jnp.int32, (CHUNK, CHUNK), 0)
    si = lax.broadcasted_iota(jnp.int32, (CHUNK, CHUNK), 1)
    tril = jnp.where(si <= ti, 1.0, 0.0).astype(BF16)
    t2 = lax.broadcasted_iota(jnp.int32, (CHUNK, 2 * CHUNK), 0)
    l2 = lax.broadcasted_iota(jnp.int32, (CHUNK, 2 * CHUNK), 1)
    keep_fwd = l2 <= t2
    keep_bwd = l2 - CHUNK > t2
    scale = dk ** -0.5
    ng = ng_ref[...]
    for c in range(tm // CHUNK):
        rows = slice(c * CHUNK, (c + 1) * CHUNK)
        zg = _dot(gl_ref[rows, :], wg_ref[...]) + bg_ref[...]
        log_a = (jnp.minimum(zg, 0.0) - jnp.log1p(jnp.exp(-jnp.abs(zg)))) * (1.0 / GLA_GATE_TEMP)
        parts = _dot(tril, jnp.concatenate(_split3(log_a), axis=1))
        b = parts[:, :hk] + parts[:, hk:2 * hk] + parts[:, 2 * hk:]
        b_last = b[CHUNK - 1:CHUNK, :]
        e_pos = jnp.exp(b)
        e_neg = jnp.exp(-b)
        q = q_ref[rows, :].astype(F32) * scale
        k = k_ref[rows, :].astype(F32)
        q_fwd = (q * e_pos).astype(BF16)
        k_fwd = (k * e_neg).astype(BF16)
        q_bwd = (q * e_neg).astype(BF16)
        k_bwd = (k * e_pos).astype(BF16)
        k_state = (k * jnp.exp(b_last - b)).astype(BF16)
        decay = jnp.transpose(jnp.broadcast_to(jnp.exp(b_last), (LANE, hk)))
        for h in range(heads):
            kc = slice(h * dk, (h + 1) * dk)
            vc = slice(h * dv, (h + 1) * dv)
            v = v_ref[rows, vc]
            s2 = _dot_nt(jnp.concatenate([q_fwd[:, kc], q_bwd[:, kc]], axis=0),
                         jnp.concatenate([k_fwd[:, kc], k_bwd[:, kc]], axis=0))
            scores = jnp.where(keep_fwd, s2[:CHUNK, :], jnp.where(keep_bwd, s2[CHUNK:, :], 0.0))
            st = st_ref[h]
            o = _dot(jnp.concatenate([q_fwd[:, kc], scores.astype(BF16)], axis=1),
                     jnp.concatenate([st.astype(BF16), v, v], axis=0))
            st_ref[h] = (st * jnp.concatenate([decay[kc, :]] * (dv // LANE), axis=1)
                         + _dot_tn(k_state[:, kc], v))
            ms = jnp.mean(o * o, axis=-1, keepdims=True)
            o = o * lax.rsqrt(ms + EPS) * ng
            o_ref[rows, vc] = (o * _silu(r_ref[rows, vc].astype(F32))).astype(o_ref.dtype)


def _gla_core(qkvr, glow, w_gate, b_gate, norm_g, *, bsz, seq, tm=None):
    heads = GLA_HEADS
    hk = w_gate.shape[1]
    dk = hk // heads
    dv = norm_g.shape[0]
    hv = heads * dv
    assert qkvr.shape[1] == 2 * hk + 2 * hv and hv % hk == 0
    tm = tm or _pick(seq, (256, 128, 64))
    tps = seq // tm
    row = lambda b, t: b * tps + t
    return pl.pallas_call(
        functools.partial(_gla_kernel, heads=heads, dk=dk, dv=dv, tm=tm),
        grid=(bsz, tps),
        in_specs=[
            pl.BlockSpec((tm, hk), lambda b, t: (row(b, t), 0)),
            pl.BlockSpec((tm, hk), lambda b, t: (row(b, t), 1)),
            pl.BlockSpec((tm, hv), lambda b, t: (row(b, t), (2 * hk) // hv)),
            pl.BlockSpec((tm, hv), lambda b, t: (row(b, t), (2 * hk) // hv + 1)),
            pl.BlockSpec((tm, LANE), lambda b, t: (row(b, t), 0)),
            pl.BlockSpec((LANE, hk), lambda b, t: (0, 0)),
            pl.BlockSpec((1, hk), lambda b, t: (0, 0)),
            pl.BlockSpec((1, dv), lambda b, t: (0, 0)),
        ],
        out_specs=pl.BlockSpec((tm, hv), lambda b, t: (row(b, t), 0)),
        out_shape=jax.ShapeDtypeStruct((bsz * seq, hv), BF16),
        scratch_shapes=[pltpu.VMEM((heads, dk, dv), F32)],
        compiler_params=_cparams(2), name="gla_core",
    )(qkvr, qkvr, qkvr, qkvr, glow, w_gate, b_gate.reshape(1, hk), norm_g.reshape(1, dv))


def _ret_kernel(q_ref, k_ref, v_ref, g_ref, cos_ref, sin_ref, o_ref, st_ref, *, heads, dk, dv, tm):
    @pl.when(pl.program_id(1) == 0)
    def _():
        st_ref[...] = jnp.zeros_like(st_ref)

    half = dk // 2
    ti = lax.broadcasted_iota(jnp.int32, (tm, tm), 0)
    si = lax.broadcasted_iota(jnp.int32, (tm, tm), 1)
    visible = si // CHUNK <= ti // CHUNK
    dist = jnp.abs((ti - si).astype(F32))
    pos = lax.broadcasted_iota(jnp.int32, (tm, LANE), 0).astype(F32)
    kscale = dk ** -0.5
    cos = cos_ref[...]
    sin = sin_ref[...]

    def rot(t):
        t1 = t[:, :half]
        t2 = t[:, half:]
        return jnp.concatenate([t1 * cos - t2 * sin, t1 * sin + t2 * cos], axis=1)

    for h in range(heads):
        log_g = math.log1p(-(2.0 ** (-5.0 - h)))
        weight = jnp.where(visible, jnp.exp(log_g * dist), 0.0)
        q_dec = jnp.concatenate([jnp.exp(log_g * (pos + 1.0))] * (dk // LANE), axis=1)
        k_dec = jnp.concatenate([jnp.exp(log_g * (tm - 1.0 - pos))] * (dk // LANE), axis=1)
        tile_decay = math.exp(log_g * tm)
        kc = slice(h * dk, (h + 1) * dk)
        vc = slice(h * dv, (h + 1) * dv)
        q = rot(q_ref[:, kc].astype(F32))
        k = rot(k_ref[:, kc].astype(F32)) * kscale
        v = v_ref[:, vc]
        scores = _dot_nt(q.astype(BF16), k.astype(BF16)) * weight
        st = st_ref[h]
        o = _dot(jnp.concatenate([scores.astype(BF16), (q * q_dec).astype(BF16)], axis=1),
                 jnp.concatenate([v, st.astype(BF16)], axis=0))
        st_ref[h] = st * tile_decay + _dot_tn((k * k_dec).astype(BF16), v)
        ms = jnp.mean(o * o, axis=-1, keepdims=True)
        o = o * lax.rsqrt(ms + EPS)
        o_ref[:, vc] = (o * _silu(g_ref[:, vc].astype(F32))).astype(o_ref.dtype)


def _ret_core(qkvg, *, bsz, seq, tm=None):
    heads = RET_HEADS
    n = qkvg.shape[1]
    hk = n // 6
    hv = 2 * hk
    dk = hk // heads
    dv = hv // heads
    half = dk // 2
    tm = tm or _pick(seq, (256, 128, 64))
    tps = seq // tm
    pos = jnp.arange(seq)
    inv_freq = ROPE_BASE ** (-jnp.arange(half, dtype=F32) / half)
    ang = pos.astype(F32)[:, None] * inv_freq
    cos = jnp.cos(ang)
    sin = jnp.sin(ang)
    row = lambda b, t: b * tps + t
    return pl.pallas_call(
        functools.partial(_ret_kernel, heads=heads, dk=dk, dv=dv, tm=tm),
        grid=(bsz, tps),
        in_specs=[
            pl.BlockSpec((tm, hk), lambda b, t: (row(b, t), 0)),
            pl.BlockSpec((tm, hk), lambda b, t: (row(b, t), 1)),
            pl.BlockSpec((tm, hv), lambda b, t: (row(b, t), 1)),
            pl.BlockSpec((tm, hv), lambda b, t: (row(b, t), 2)),
            pl.BlockSpec((tm, half), lambda b, t: (t, 0)),
            pl.BlockSpec((tm, half), lambda b, t: (t, 0)),
        ],
        out_specs=pl.BlockSpec((tm, hv), lambda b, t: (row(b, t), 0)),
        out_shape=jax.ShapeDtypeStruct((bsz * seq, hv), BF16),
        scratch_shapes=[pltpu.VMEM((heads, dk, dv), F32)],
        compiler_params=_cparams(2), name="ret_core",
    )(qkvg, qkvg, qkvg, qkvg, cos, sin)


def _sgu_kernel(u_ref, v_ref, lg_ref, lb_ref, ws_ref, bs_ref, o_ref, vn_ref, *, heads, hd, tm):
    lg = lg_ref[...]
    lb = lb_ref[...]
    rc = 64
    for r in range(tm // rc):
        rows = slice(r * rc, (r + 1) * rc)
        v = v_ref[rows, :].astype(F32)
        mu = jnp.mean(v, axis=-1, keepdims=True)
        v = v - mu
        var = jnp.mean(v * v, axis=-1, keepdims=True)
        vn_ref[rows, :] = (v * lax.rsqrt(var + EPS) * lg + lb).astype(BF16)
    pi = lax.broadcasted_iota(jnp.int32, (SGU_CHUNK, SGU_CHUNK), 0)
    pj = lax.broadcasted_iota(jnp.int32, (SGU_CHUNK, SGU_CHUNK), 1)
    mask = (pj // CHUNK) <= (pi // CHUNK)
    for h in range(heads):
        w = jnp.where(mask, ws_ref[h], 0.0).astype(BF16)
        bias = bs_ref[h]
        cols = slice(h * hd, (h + 1) * hd)
        for n in range(tm // SGU_CHUNK):
            rows = slice(n * SGU_CHUNK, (n + 1) * SGU_CHUNK)
            sv = _dot(w, vn_ref[rows, cols]) + bias
            o_ref[rows, cols] = (u_ref[rows, cols].astype(F32) * sv).astype(o_ref.dtype)


def _sgu_core(uv, ln_g, ln_b, w_s, b_s, *, seq, tm=None):
    t = uv.shape[0]
    width = uv.shape[1] // 2
    heads = SGU_HEADS
    hd = width // heads
    tm = tm or _pick(seq, (256, 128))
    full2 = lambda i: (0, 0)
    full3 = lambda i: (0, 0, 0)
    return pl.pallas_call(
        functools.partial(_sgu_kernel, heads=heads, hd=hd, tm=tm),
        grid=(t // tm,),
        in_specs=[
            pl.BlockSpec((tm, width), lambda i: (i, 0)),
            pl.BlockSpec((tm, width), lambda i: (i, 1)),
            pl.BlockSpec((1, width), full2),
            pl.BlockSpec((1, width), full2),
            pl.BlockSpec((heads, SGU_CHUNK, SGU_CHUNK), full3),
            pl.BlockSpec((heads, SGU_CHUNK, 1), full3),
        ],
        out_specs=pl.BlockSpec((tm, width), lambda i: (i, 0)),
        out_shape=jax.ShapeDtypeStruct((t, width), BF16),
        scratch_shapes=[pltpu.VMEM((tm, width), BF16)],
        compiler_params=_cparams(1), name="sgu_core",
    )(uv, uv, ln_g.reshape(1, width), ln_b.reshape(1, width), w_s,
      b_s.reshape(heads, SGU_CHUNK, 1))


def kernel(x, c, ada_w, ada_b, norm_g, ffn_w_in, ffn_conv_w, ffn_conv_b, ffn_w_out, cm_w_in, cm_dw_w, cm_dw_b, cm_ln_g, cm_ln_b, cm_w_out, gla_w_in, gla_w_gate, gla_b_gate, gla_norm_g, gla_w_out, ret_w_in, ret_w_out, sgu_w_in, sgu_ln_g, sgu_ln_b, sgu_w_s, sgu_b_s, sgu_w_out):
    bsz, seq, d = x.shape
    depth = ada_w.shape[0]
    mod5 = _ada(c, ada_w, ada_b).reshape(depth, bsz, 2, 3, d)
    xf = x.reshape(bsz * seq, d)
    bf = lambda a: a.astype(BF16)
    for i in range(depth):
        m, j = i % N_MIXERS, i // N_MIXERS
        mm_in = functools.partial(_mm_in, xf, mod5, i, 0, norm_g[i, 0], seq=seq)
        post = dict(x=xf, mod5=mod5, layer=i, slot=0, g=norm_g[i, 1], seq=seq)
        if m == 0:
            z = mm_in(bf(cm_w_in[j]), kind="glu", out_dtype=F32)
            xf = _conv_out(z, cm_dw_w[j], cm_dw_b[j], cm_ln_g[j], cm_ln_b[j], bf(cm_w_out[j]), **post)
        elif m == 1:
            hk = gla_w_gate.shape[2]
            n_main = gla_w_in.shape[2] - gla_w_gate.shape[1]
            rank = gla_w_gate.shape[1]
            w_low = jnp.pad(gla_w_in[j][:, n_main:], ((0, 0), (0, LANE - rank)))
            w_gate = jnp.pad(gla_w_gate[j], ((0, LANE - rank), (0, 0)))
            qkvr = mm_in(bf(gla_w_in[j][:, :n_main]), kind="plain", out_dtype=BF16)
            glow = mm_in(bf(w_low), kind="plain", out_dtype=BF16)
            o = _gla_core(qkvr, glow, bf(w_gate), gla_b_gate[j], gla_norm_g[j], bsz=bsz, seq=seq)
            xf = _mm_out(o, bf(gla_w_out[j]), **post)
        elif m == 2:
            qkvg = mm_in(bf(ret_w_in[j]), kind="plain", out_dtype=BF16)
            o = _ret_core(qkvg, bsz=bsz, seq=seq)
            xf = _mm_out(o, bf(ret_w_out[j]), **post)
        else:
            uv = mm_in(bf(sgu_w_in[j]), kind="gelu", out_dtype=BF16)
            o = _sgu_core(uv, sgu_ln_g[j], sgu_ln_b[j], sgu_w_s[j], sgu_b_s[j], seq=seq)
            xf = _mm_out(o, bf(sgu_w_out[j]), **post)
        u = _mm_in(xf, mod5, i, 1, norm_g[i, 2], bf(ffn_w_in[i]), seq=seq, kind="convffn",
                   out_dtype=BF16, conv_w=ffn_conv_w[i], conv_b=ffn_conv_b[i])
        xf = _mm_out(u, bf(ffn_w_out[i]), xf, mod5, i, 1, norm_g[i, 3], seq=seq)
    return xf.reshape(bsz, seq, d)
```

```python
import functools
import math

import jax
import jax.numpy as jnp
from jax import lax
from jax.experimental import pallas as pl
from jax.experimental.pallas import tpu as pltpu

F32 = jnp.float32
BF16 = jnp.bfloat16

EPS = 1e-6
CHUNK = 64
N_MIXERS = 4
N_MOD = 6
GLA_HEADS = 4
GLA_GATE_TEMP = 16.0
RET_HEADS = 8
ROPE_BASE = 10000.0
SGU_CHUNK = 128
SGU_HEADS = 8

LANE = 128
BF16_ROWS = 16
CONV_HALO = 32
VMEM_LIMIT = 60 * 1024 * 1024


def _cparams(n_axes):
    return pltpu.CompilerParams(
        dimension_semantics=("arbitrary",) * n_axes,
        vmem_limit_bytes=VMEM_LIMIT)


def _sigmoid(x):
    return 1.0 / (1.0 + jnp.exp(-x))


def _silu(x):
    return x * _sigmoid(x)


def _dot(a, b):
    return jnp.dot(a, b, preferred_element_type=F32)


def _dot_nt(a, b):
    return lax.dot_general(a, b, (((1,), (1,)), ((), ())), preferred_element_type=F32)


def _dot_tn(a, b):
    return lax.dot_general(a, b, (((0,), (0,)), ((), ())), preferred_element_type=F32)


def _pick(n, candidates):
    for c in candidates:
        if n % c == 0:
            return c
    raise ValueError(f"no tile for {n} in {candidates}")


def _ada_kernel(c_ref, w_ref, b_ref, o_ref):
    ca = _silu(c_ref[...]).astype(BF16)
    o_ref[...] = _dot(ca, w_ref[...].astype(BF16)) + b_ref[...]


def _ada(c, ada_w, ada_b):
    depth, d, n = ada_w.shape
    bsz = c.shape[0]
    tn = _pick(n, (1536, 1024, 512, 256, 128))
    return pl.pallas_call(
        _ada_kernel,
        grid=(depth, n // tn),
        in_specs=[
            pl.BlockSpec((bsz, d), lambda l, j: (0, 0)),
            pl.BlockSpec((None, d, tn), lambda l, j: (l, 0, j)),
            pl.BlockSpec((None, 1, tn), lambda l, j: (l, 0, j)),
        ],
        out_specs=pl.BlockSpec((None, bsz, tn), lambda l, j: (l, 0, j)),
        out_shape=jax.ShapeDtypeStruct((depth, bsz, n), F32),
        compiler_params=_cparams(2),
        name="ada_proj",
    )(c, ada_w, ada_b.reshape(depth, 1, n))


def _modnorm(x, gs, shift):
    ms = jnp.mean(x * x, axis=-1, keepdims=True)
    return (x * lax.rsqrt(ms + EPS) * gs + shift).astype(BF16)


def _modnorm_kernel(x_ref, mod_ref, g_ref, h_ref, *, tm):
    gs = g_ref[...] * (1.0 + mod_ref[1:2, :])
    shift = mod_ref[0:1, :]
    rp = BF16_ROWS
    for r in range(tm // rp):
        rows = slice(r * rp, (r + 1) * rp)
        h_ref[rows, :] = _modnorm(x_ref[rows, :], gs, shift)


def _modnorm_call(x, mod5, layer, slot, g, *, seq, tm=None):
    t, d = x.shape
    tm = tm or _pick(seq, (512, 256, 128))
    tps = seq // tm
    return pl.pallas_call(
        functools.partial(_modnorm_kernel, tm=tm),
        grid=(t // tm,),
        in_specs=[
            pl.BlockSpec((tm, d), lambda i: (i, 0)),
            pl.BlockSpec((None, None, None, 3, d), lambda i: (layer, i // tps, slot, 0, 0)),
            pl.BlockSpec((1, d), lambda i: (0, 0)),
        ],
        out_specs=pl.BlockSpec((tm, d), lambda i: (i, 0)),
        out_shape=jax.ShapeDtypeStruct((t, d), BF16),
        compiler_params=_cparams(1), name="modnorm",
    )(x, mod5, g.reshape(1, d))


def _cast_w(w_ref, wb_ref):
    step = 256
    for r in range(w_ref.shape[0] // step):
        rows = slice(r * step, (r + 1) * step)
        wb_ref[rows, :] = w_ref[rows, :].astype(BF16)


def _mm_in_kernel(h_ref, w_ref, o_ref, wb_ref, *, tm, rc, act, silu_from):
    j = pl.program_id(0)

    @pl.when(pl.program_id(1) == 0)
    def _():
        _cast_w(w_ref, wb_ref)

    def body(gate):
        for r in range(tm // rc):
            rows = slice(r * rc, (r + 1) * rc)
            y = _dot(h_ref[rows, :], wb_ref[...])
            if act == "gelu":
                y = 0.5 * y * (1.0 + lax.erf(y * (1.0 / math.sqrt(2.0))))
            if gate:
                y = _silu(y)
            o_ref[rows, :] = y.astype(o_ref.dtype)

    if silu_from is None:
        body(False)
    else:
        pl.when(j < silu_from)(lambda: body(False))
        pl.when(j >= silu_from)(lambda: body(True))


def _mm_glu_kernel(h_ref, wa_ref, wg_ref, o_ref, wab_ref, wgb_ref, *, tm, rc):
    @pl.when(pl.program_id(1) == 0)
    def _():
        _cast_w(wa_ref, wab_ref)
        _cast_w(wg_ref, wgb_ref)

    for r in range(tm // rc):
        rows = slice(r * rc, (r + 1) * rc)
        a = _dot(h_ref[rows, :], wab_ref[...])
        gate = _dot(h_ref[rows, :], wgb_ref[...])
        o_ref[rows, :] = (a * _sigmoid(gate)).astype(o_ref.dtype)


def _mm_convffn_kernel(h_ref, wa_ref, wg_ref, cw_ref, cb_ref, o_ref, wab_ref, wgb_ref, tail_ref,
                       *, tm, rc, tiles_per_seq):
    hr = 8
    i = pl.program_id(1)

    @pl.when(i == 0)
    def _():
        _cast_w(wa_ref, wab_ref)
        _cast_w(wg_ref, wgb_ref)

    @pl.when(i % tiles_per_seq == 0)
    def _():
        tail_ref[...] = jnp.zeros_like(tail_ref)

    w0 = cw_ref[0:1, :]
    w1 = cw_ref[1:2, :]
    w2 = cw_ref[2:3, :]
    cb = cb_ref[...]
    tail = tail_ref[...]
    for r in range(tm // rc):
        rows = slice(r * rc, (r + 1) * rc)
        a = _dot(h_ref[rows, :], wab_ref[...])
        b = _dot(h_ref[rows, :], wgb_ref[...])
        af = jnp.concatenate([tail, a], axis=0)
        a1 = pltpu.roll(af, 1, 0)[hr:, :]
        a2 = pltpu.roll(af, 2, 0)[hr:, :]
        conv = a2 * w0 + a1 * w1 + a * w2 + cb
        o_ref[rows, :] = (_silu(conv) * b).astype(o_ref.dtype)
        tail = a[rc - hr:, :]
    tail_ref[...] = tail


def _mm_in(h, w, widx, n_out, *, seq, kind, out_dtype, conv_w=None, conv_b=None,
           silu_cols=None, tm=None, tn=None, rc=256):
    t, d = h.shape
    tm = tm or _pick(seq, (1024, 512, 256, 128))
    tn = tn or _pick(n_out, (512, 256, 128) if kind in ("glu", "convffn") else (1024, 512, 256, 128))
    rc = min(rc, tm)
    tps = seq // tm
    nb = n_out // tn
    grid = (nb, t // tm)
    h_spec = pl.BlockSpec((tm, d), lambda j, i: (i, 0))
    wa_spec = pl.BlockSpec((None, d, tn), lambda j, i: (widx, 0, j))
    wg_spec = pl.BlockSpec((None, d, tn), lambda j, i: (widx, 0, j + nb))
    out_spec = pl.BlockSpec((tm, tn), lambda j, i: (i, j))
    out_shape = jax.ShapeDtypeStruct((t, n_out), out_dtype)
    w_scratch = pltpu.VMEM((d, tn), BF16)
    if kind in ("plain", "gelu"):
        silu_from = None
        if silu_cols is not None:
            assert silu_cols % tn == 0
            silu_from = silu_cols // tn
        return pl.pallas_call(
            functools.partial(_mm_in_kernel, tm=tm, rc=rc, act=kind, silu_from=silu_from),
            grid=grid,
            in_specs=[h_spec, wa_spec],
            out_specs=out_spec, out_shape=out_shape,
            scratch_shapes=[w_scratch],
            compiler_params=_cparams(2), name=f"mm_in_{kind}",
        )(h, w)
    if kind == "glu":
        return pl.pallas_call(
            functools.partial(_mm_glu_kernel, tm=tm, rc=rc),
            grid=grid,
            in_specs=[h_spec, wa_spec, wg_spec],
            out_specs=out_spec, out_shape=out_shape,
            scratch_shapes=[w_scratch, w_scratch],
            compiler_params=_cparams(2), name="mm_in_glu",
        )(h, w, w)
    assert kind == "convffn"
    kw = conv_w.shape[1]
    assert kw == 3
    return pl.pallas_call(
        functools.partial(_mm_convffn_kernel, tm=tm, rc=rc, tiles_per_seq=tps),
        grid=grid,
        in_specs=[h_spec, wa_spec, wg_spec,
                  pl.BlockSpec((None, kw, tn), lambda j, i: (widx, 0, j)),
                  pl.BlockSpec((None, 1, tn), lambda j, i: (widx, 0, j))],
        out_specs=out_spec, out_shape=out_shape,
        scratch_shapes=[w_scratch, w_scratch, pltpu.VMEM((8, tn), F32)],
        compiler_params=_cparams(2), name="mm_in_convffn",
    )(h, w, w, conv_w, conv_b.reshape(conv_b.shape[0], 1, n_out))


def _post(y, x, g, mod):
    ms = jnp.mean(y * y, axis=-1, keepdims=True)
    return x + mod[2:3, :] * (y * lax.rsqrt(ms + EPS) * g)


def _next_specs(nxt, mod5, d, tm, tps, t):
    if nxt is None:
        return [], [], [], []
    layer_n, slot_n, g_n = nxt
    in_specs = [pl.BlockSpec((None, None, None, 3, d), lambda i: (layer_n, i // tps, slot_n, 0, 0)),
                pl.BlockSpec((1, d), lambda i: (0, 0))]
    return (in_specs, [mod5, g_n.reshape(1, d)],
            [pl.BlockSpec((tm, d), lambda i: (i, 0))], [jax.ShapeDtypeStruct((t, d), BF16)])


def _mm_out_kernel(u_ref, w_ref, x_ref, mod_ref, g_ref, *rest, tm, rc, has_next):
    if has_next:
        modn_ref, gn_ref, o_ref, hn_ref = rest
        gs = gn_ref[...] * (1.0 + modn_ref[1:2, :])
        shift = modn_ref[0:1, :]
    else:
        (o_ref,) = rest
    g = g_ref[...]
    mod = mod_ref[...]
    for r in range(tm // rc):
        rows = slice(r * rc, (r + 1) * rc)
        y = _dot(u_ref[rows, :], w_ref[...])
        xn = _post(y, x_ref[rows, :], g, mod)
        o_ref[rows, :] = xn
        if has_next:
            hn_ref[rows, :] = _modnorm(xn, gs, shift)


def _mm_out(u, w, x, mod5, layer, slot, g, *, seq, nxt=None, tm=None, rc=128):
    t, k = u.shape
    d = w.shape[1]
    tm = tm or _pick(seq, (512, 256, 128))
    rc = min(rc, tm)
    tps = seq // tm
    n_in, n_args, n_out, n_shape = _next_specs(nxt, mod5, d, tm, tps, t)
    res = pl.pallas_call(
        functools.partial(_mm_out_kernel, tm=tm, rc=rc, has_next=nxt is not None),
        grid=(t // tm,),
        in_specs=[
            pl.BlockSpec((tm, k), lambda i: (i, 0)),
            pl.BlockSpec((k, d), lambda i: (0, 0), pipeline_mode=pl.Buffered(1)),
            pl.BlockSpec((tm, d), lambda i: (i, 0)),
            pl.BlockSpec((None, None, None, 3, d), lambda i: (layer, i // tps, slot, 0, 0)),
            pl.BlockSpec((1, d), lambda i: (0, 0)),
        ] + n_in,
        out_specs=[pl.BlockSpec((tm, d), lambda i: (i, 0))] + n_out,
        out_shape=[jax.ShapeDtypeStruct((t, d), F32)] + n_shape,
        compiler_params=_cparams(1), name="mm_out",
    )(u, w, x, mod5, g.reshape(1, d), *n_args)
    return res if nxt is not None else res[0]


def _conv_out_kernel(z_ref, zh_ref, dw_ref, db_ref, lg_ref, lb_ref, w_ref, x_ref, mod_ref, g_ref,
                     modn_ref, gn_ref, o_ref, hn_ref, zf_ref, zc_ref, hb_ref,
                     *, tm, kw, rc, cc, rl, tiles_per_seq):
    d = z_ref.shape[1]
    halo = CONV_HALO
    sub = 8
    at_start = pl.program_id(0) % tiles_per_seq == 0
    zf_ref[0:halo, :] = jnp.where(at_start, 0.0, zh_ref[...])
    zf_ref[halo:, :] = z_ref[...]
    base = halo - (kw - 1)
    n = rc + halo
    for r in range(tm // rc):
        for c in range(d // cc):
            cols = slice(c * cc, (c + 1) * cc)
            blk = zf_ref[r * rc:r * rc + n, cols]
            acc = jnp.broadcast_to(db_ref[:, cols], (rc, cc))
            for s in range(sub):
                sh = blk if s == 0 else pltpu.roll(blk, n - s, 0)
                for q in range(halo // sub + 1):
                    k = sub * q + s - base
                    if 0 <= k < kw:
                        acc = acc + sh[sub * q:sub * q + rc, :] * dw_ref[k:k + 1, cols]
            zc_ref[r * rc:(r + 1) * rc, cols] = acc
    lg = lg_ref[...]
    lb = lb_ref[...]
    for r in range(tm // rl):
        rows = slice(r * rl, (r + 1) * rl)
        zc = zc_ref[rows, :]
        mu = jnp.mean(zc, axis=-1, keepdims=True)
        zc = zc - mu
        var = jnp.mean(zc * zc, axis=-1, keepdims=True)
        y = zc * lax.rsqrt(var + EPS) * lg + lb
        hb_ref[rows, :] = _silu(y).astype(BF16)
    g = g_ref[...]
    mod = mod_ref[...]
    gs = gn_ref[...] * (1.0 + modn_ref[1:2, :])
    shift = modn_ref[0:1, :]
    ro = 128
    for r in range(tm // ro):
        rows = slice(r * ro, (r + 1) * ro)
        y = _dot(hb_ref[rows, :], w_ref[...])
        xn = _post(y, x_ref[rows, :], g, mod)
        o_ref[rows, :] = xn
        hn_ref[rows, :] = _modnorm(xn, gs, shift)


def _conv_out(z, dw_w, dw_b, ln_g, ln_b, w, x, mod5, layer, slot, g, *, seq, nxt, tm=None):
    t, d = z.shape
    kw = dw_w.shape[0]
    assert kw - 1 <= CONV_HALO
    tm = tm or _pick(seq, (256, 128))
    tps = seq // tm
    hb = tm // CONV_HALO
    full = lambda i: (0, 0)
    n_in, n_args, n_out, n_shape = _next_specs(nxt, mod5, d, tm, tps, t)
    return pl.pallas_call(
        functools.partial(_conv_out_kernel, tm=tm, kw=kw, rc=tm, cc=LANE, rl=32,
                          tiles_per_seq=tps),
        grid=(t // tm,),
        in_specs=[
            pl.BlockSpec((tm, d), lambda i: (i, 0)),
            pl.BlockSpec((CONV_HALO, d), lambda i: (jnp.maximum(i * hb - 1, 0), 0)),
            pl.BlockSpec((kw, d), full),
            pl.BlockSpec((1, d), full),
            pl.BlockSpec((1, d), full),
            pl.BlockSpec((1, d), full),
            pl.BlockSpec((d, d), full, pipeline_mode=pl.Buffered(1)),
            pl.BlockSpec((tm, d), lambda i: (i, 0)),
            pl.BlockSpec((None, None, None, 3, d), lambda i: (layer, i // tps, slot, 0, 0)),
            pl.BlockSpec((1, d), full),
        ] + n_in,
        out_specs=[pl.BlockSpec((tm, d), lambda i: (i, 0))] + n_out,
        out_shape=[jax.ShapeDtypeStruct((t, d), F32)] + n_shape,
        scratch_shapes=[pltpu.VMEM((tm + CONV_HALO, d), F32), pltpu.VMEM((tm, d), F32),
                        pltpu.VMEM((tm, d), BF16)],
        compiler_params=_cparams(1), name="conv_out",
    )(z, z, dw_w, dw_b.reshape(1, d), ln_g.reshape(1, d), ln_b.reshape(1, d), w, x, mod5,
      g.reshape(1, d), *n_args)


def _split3(a):
    hi = a.astype(BF16)
    r1 = a - hi.astype(F32)
    mid = r1.astype(BF16)
    lo = (r1 - mid.astype(F32)).astype(BF16)
    return hi, mid, lo


def _gla_kernel(q_ref, k_ref, v_ref, r_ref, gl_ref, wg_ref, bg_ref, ng_ref, o_ref, st_ref,
                *, heads, dk, dv, tm):
    @pl.when(pl.program_id(1) == 0)
    def _():
        st_ref[...] = jnp.zeros_like(st_ref)

    hk = heads * dk
    ti = lax.broadcasted_iota(jnp.int32, (CHUNK, CHUNK), 0)
    si = lax.broadcasted_iota(jnp.int32, (CHUNK, CHUNK), 1)
    tril = jnp.where(si <= ti, 1.0, 0.0).astype(BF16)
    t2 = lax.broadcasted_iota(jnp.int32, (CHUNK, 2 * CHUNK), 0)
    l2 = lax.broadcasted_iota(jnp.int32, (CHUNK, 2 * CHUNK), 1)
    keep_fwd = l2 <= t2
    keep_bwd = l2 - CHUNK > t2
    scale = dk ** -0.5
    ng = ng_ref[...]
    for c in range(tm // CHUNK):
        rows = slice(c * CHUNK, (c + 1) * CHUNK)
        zg = _dot(gl_ref[rows, :], wg_ref[...]) + bg_ref[...]
        log_a = (jnp.minimum(zg, 0.0) - jnp.log1p(jnp.exp(-jnp.abs(zg)))) * (1.0 / GLA_GATE_TEMP)
        parts = _dot(tril, jnp.concatenate(_split3(log_a), axis=1))
        b = parts[:, :hk] + parts[:, hk:2 * hk] + parts[:, 2 * hk:]
        b_last = b[CHUNK - 1:CHUNK, :]
        e_pos = jnp.exp(b)
        e_neg = jnp.exp(-b)
        q = q_ref[rows, :].astype(F32) * scale
        k = k_ref[rows, :].astype(F32)
        q_fwd = (q * e_pos).astype(BF16)
        k_fwd = (k * e_neg).astype(BF16)
        q_bwd = (q * e_neg).astype(BF16)
        k_bwd = (k * e_pos).astype(BF16)
        k_state = (k * jnp.exp(b_last - b)).astype(BF16)
        decay = jnp.transpose(jnp.broadcast_to(jnp.exp(b_last), (LANE, hk)))
        for h in range(heads):
            kc = slice(h * dk, (h + 1) * dk)
            vc = slice(h * dv, (h + 1) * dv)
            v = v_ref[rows, vc]
            s2 = _dot_nt(jnp.concatenate([q_fwd[:, kc], q_bwd[:, kc]], axis=0),
                         jnp.concatenate([k_fwd[:, kc], k_bwd[:, kc]], axis=0))
            scores = jnp.where(keep_fwd, s2[:CHUNK, :], jnp.where(keep_bwd, s2[CHUNK:, :], 0.0))
            st = st_ref[h]
            o = _dot(jnp.concatenate([q_fwd[:, kc], scores.astype(BF16)], axis=1),
                     jnp.concatenate([st.astype(BF16), v, v], axis=0))
            st_ref[h] = (st * jnp.concatenate([decay[kc, :]] * (dv // LANE), axis=1)
                         + _dot_tn(k_state[:, kc], v))
            ms = jnp.mean(o * o, axis=-1, keepdims=True)
            o = o * lax.rsqrt(ms + EPS) * ng
            o_ref[rows, vc] = (o * r_ref[rows, vc].astype(F32)).astype(o_ref.dtype)


def _gla_core(qkvr, glow, w_gate, b_gate, norm_g, *, bsz, seq, tm=None):
    heads = GLA_HEADS
    hk = w_gate.shape[1]
    dk = hk // heads
    dv = norm_g.shape[0]
    hv = heads * dv
    assert qkvr.shape[1] == 2 * hk + 2 * hv and hv % hk == 0
    tm = tm or _pick(seq, (256, 128, 64))
    tps = seq // tm
    row = lambda b, t: b * tps + t
    return pl.pallas_call(
        functools.partial(_gla_kernel, heads=heads, dk=dk, dv=dv, tm=tm),
        grid=(bsz, tps),
        in_specs=[
            pl.BlockSpec((tm, hk), lambda b, t: (row(b, t), 0)),
            pl.BlockSpec((tm, hk), lambda b, t: (row(b, t), 1)),
            pl.BlockSpec((tm, hv), lambda b, t: (row(b, t), (2 * hk) // hv)),
            pl.BlockSpec((tm, hv), lambda b, t: (row(b, t), (2 * hk) // hv + 1)),
            pl.BlockSpec((tm, LANE), lambda b, t: (row(b, t), 0)),
            pl.BlockSpec((LANE, hk), lambda b, t: (0, 0)),
            pl.BlockSpec((1, hk), lambda b, t: (0, 0)),
            pl.BlockSpec((1, dv), lambda b, t: (0, 0)),
        ],
        out_specs=pl.BlockSpec((tm, hv), lambda b, t: (row(b, t), 0)),
        out_shape=jax.ShapeDtypeStruct((bsz * seq, hv), BF16),
        scratch_shapes=[pltpu.VMEM((heads, dk, dv), F32)],
        compiler_params=_cparams(2), name="gla_core",
    )(qkvr, qkvr, qkvr, qkvr, glow, w_gate, b_gate.reshape(1, hk), norm_g.reshape(1, dv))


def _ret_kernel(q_ref, k_ref, v_ref, g_ref, cos_ref, sin_ref, o_ref, st_ref, *, heads, dk, dv, tm):
    @pl.when(pl.program_id(1) == 0)
    def _():
        st_ref[...] = jnp.zeros_like(st_ref)

    half = dk // 2
    ti = lax.broadcasted_iota(jnp.int32, (tm, tm), 0)
    si = lax.broadcasted_iota(jnp.int32, (tm, tm), 1)
    visible = si // CHUNK <= ti // CHUNK
    dist = jnp.abs((ti - si).astype(F32))
    pos = lax.broadcasted_iota(jnp.int32, (tm, LANE), 0).astype(F32)
    kscale = dk ** -0.5
    cos = cos_ref[...]
    sin = sin_ref[...]

    def rot(t):
        t1 = t[:, :half]
        t2 = t[:, half:]
        return jnp.concatenate([t1 * cos - t2 * sin, t1 * sin + t2 * cos], axis=1)

    for h in range(heads):
        log_g = math.log1p(-(2.0 ** (-5.0 - h)))
        weight = jnp.where(visible, jnp.exp(log_g * dist), 0.0)
        q_dec = jnp.concatenate([jnp.exp(log_g * (pos + 1.0))] * (dk // LANE), axis=1)
        k_dec = jnp.concatenate([jnp.exp(log_g * (tm - 1.0 - pos))] * (dk // LANE), axis=1)
        tile_decay = math.exp(log_g * tm)
        kc = slice(h * dk, (h + 1) * dk)
        vc = slice(h * dv, (h + 1) * dv)
        q = rot(q_ref[:, kc].astype(F32))
        k = rot(k_ref[:, kc].astype(F32)) * kscale
        v = v_ref[:, vc]
        scores = _dot_nt(q.astype(BF16), k.astype(BF16)) * weight
        st = st_ref[h]
        o = _dot(jnp.concatenate([scores.astype(BF16), (q * q_dec).astype(BF16)], axis=1),
                 jnp.concatenate([v, st.astype(BF16)], axis=0))
        st_ref[h] = st * tile_decay + _dot_tn((k * k_dec).astype(BF16), v)
        ms = jnp.mean(o * o, axis=-1, keepdims=True)
        o = o * lax.rsqrt(ms + EPS)
        o_ref[:, vc] = (o * g_ref[:, vc].astype(F32)).astype(o_ref.dtype)


def _ret_core(qkvg, *, bsz, seq, tm=None):
    heads = RET_HEADS
    n = qkvg.shape[1]
    hk = n // 6
    hv = 2 * hk
    dk = hk // heads
    dv = hv // heads
    half = dk // 2
    tm = tm or _pick(seq, (256, 128, 64))
    tps = seq // tm
    pos = jnp.arange(seq)
    inv_freq = ROPE_BASE ** (-jnp.arange(half, dtype=F32) / half)
    ang = pos.astype(F32)[:, None] * inv_freq
    cos = jnp.cos(ang)
    sin = jnp.sin(ang)
    row = lambda b, t: b * tps + t
    return pl.pallas_call(
        functools.partial(_ret_kernel, heads=heads, dk=dk, dv=dv, tm=tm),
        grid=(bsz, tps),
        in_specs=[
            pl.BlockSpec((tm, hk), lambda b, t: (row(b, t), 0)),
            pl.BlockSpec((tm, hk), lambda b, t: (row(b, t), 1)),
            pl.BlockSpec((tm, hv), lambda b, t: (row(b, t), 1)),
            pl.BlockSpec((tm, hv), lambda b, t: (row(b, t), 2)),
            pl.BlockSpec((tm, half), lambda b, t: (t, 0)),
            pl.BlockSpec((tm, half), lambda b, t: (t, 0)),
        ],
        out_specs=pl.BlockSpec((tm, hv), lambda b, t: (row(b, t), 0)),
        out_shape=jax.ShapeDtypeStruct((bsz * seq, hv), BF16),
        scratch_shapes=[pltpu.VMEM((heads, dk, dv), F32)],
        compiler_params=_cparams(2), name="ret_core",
    )(qkvg, qkvg, qkvg, qkvg, cos, sin)


def _sgu_kernel(u_ref, v_ref, lg_ref, lb_ref, ws_ref, bs_ref, o_ref, vn_ref, *, heads, hd, tm):
    lg = lg_ref[...]
    lb = lb_ref[...]
    rc = 64
    for r in range(tm // rc):
        rows = slice(r * rc, (r + 1) * rc)
        v = v_ref[rows, :].astype(F32)
        mu = jnp.mean(v, axis=-1, keepdims=True)
        v = v - mu
        var = jnp.mean(v * v, axis=-1, keepdims=True)
        vn_ref[rows, :] = (v * lax.rsqrt(var + EPS) * lg + lb).astype(BF16)
    pi = lax.broadcasted_iota(jnp.int32, (SGU_CHUNK, SGU_CHUNK), 0)
    pj = lax.broadcasted_iota(jnp.int32, (SGU_CHUNK, SGU_CHUNK), 1)
    mask = (pj // CHUNK) <= (pi // CHUNK)
    for h in range(heads):
        w = jnp.where(mask, ws_ref[h], 0.0).astype(BF16)
        bias = bs_ref[h]
        cols = slice(h * hd, (h + 1) * hd)
        for n in range(tm // SGU_CHUNK):
            rows = slice(n * SGU_CHUNK, (n + 1) * SGU_CHUNK)
            sv = _dot(w, vn_ref[rows, cols]) + bias
            o_ref[rows, cols] = (u_ref[rows, cols].astype(F32) * sv).astype(o_ref.dtype)


def _sgu_core(uv, ln_g, ln_b, w_s, b_s, *, seq, tm=None):
    t = uv.shape[0]
    width = uv.shape[1] // 2
    heads = SGU_HEADS
    hd = width // heads
    tm = tm or _pick(seq, (256, 128))
    full2 = lambda i: (0, 0)
    full3 = lambda i: (0, 0, 0)
    return pl.pallas_call(
        functools.partial(_sgu_kernel, heads=heads, hd=hd, tm=tm),
        grid=(t // tm,),
        in_specs=[
            pl.BlockSpec((tm, width), lambda i: (i, 0)),
            pl.BlockSpec((tm, width), lambda i: (i, 1)),
            pl.BlockSpec((1, width), full2),
            pl.BlockSpec((1, width), full2),
            pl.BlockSpec((heads, SGU_CHUNK, SGU_CHUNK), full3),
            pl.BlockSpec((heads, SGU_CHUNK, 1), full3),
        ],
        out_specs=pl.BlockSpec((tm, width), lambda i: (i, 0)),
        out_shape=jax.ShapeDtypeStruct((t, width), BF16),
        scratch_shapes=[pltpu.VMEM((tm, width), BF16)],
        compiler_params=_cparams(1), name="sgu_core",
    )(uv, uv, ln_g.reshape(1, width), ln_b.reshape(1, width), w_s,
      b_s.reshape(heads, SGU_CHUNK, 1))


def kernel(x, c, ada_w, ada_b, norm_g, ffn_w_in, ffn_conv_w, ffn_conv_b, ffn_w_out, cm_w_in, cm_dw_w, cm_dw_b, cm_ln_g, cm_ln_b, cm_w_out, gla_w_in, gla_w_gate, gla_b_gate, gla_norm_g, gla_w_out, ret_w_in, ret_w_out, sgu_w_in, sgu_ln_g, sgu_ln_b, sgu_w_s, sgu_b_s, sgu_w_out):
    bsz, seq, d = x.shape
    depth = ada_w.shape[0]
    mod5 = _ada(c, ada_w, ada_b).reshape(depth, bsz, 2, 3, d)
    xf = x.reshape(bsz * seq, d)
    bf = lambda a: a.astype(BF16)
    h = _modnorm_call(xf, mod5, 0, 0, norm_g[0, 0], seq=seq)
    for i in range(depth):
        m, j = i % N_MIXERS, i // N_MIXERS
        mm_in = functools.partial(_mm_in, h, seq=seq)
        post = dict(x=xf, mod5=mod5, layer=i, slot=0, g=norm_g[i, 1], seq=seq,
                    nxt=(i, 1, norm_g[i, 2]))
        if m == 0:
            z = mm_in(cm_w_in, j, cm_w_in.shape[2] // 2, kind="glu", out_dtype=F32)
            xf, h = _conv_out(z, cm_dw_w[j], cm_dw_b[j], cm_ln_g[j], cm_ln_b[j], bf(cm_w_out[j]), **post)
        elif m == 1:
            rank, hk = gla_w_gate.shape[1:]
            n_main = gla_w_in.shape[2] - rank
            hv = (n_main - 2 * hk) // 2
            w_low = jnp.pad(gla_w_in[j][:, n_main:], ((0, 0), (0, LANE - rank)))
            w_gate = jnp.pad(gla_w_gate[j], ((0, LANE - rank), (0, 0)))
            qkvr = mm_in(gla_w_in, j, n_main, kind="plain", out_dtype=BF16, silu_cols=2 * hk + hv)
            glow = mm_in(w_low[None], 0, LANE, kind="plain", out_dtype=BF16)
            o = _gla_core(qkvr, glow, bf(w_gate), gla_b_gate[j], gla_norm_g[j], bsz=bsz, seq=seq)
            xf, h = _mm_out(o, bf(gla_w_out[j]), **post)
        elif m == 2:
            n_ret = ret_w_in.shape[2]
            qkvg = mm_in(ret_w_in, j, n_ret, kind="plain", out_dtype=BF16, silu_cols=n_ret - n_ret // 3)
            o = _ret_core(qkvg, bsz=bsz, seq=seq)
            xf, h = _mm_out(o, bf(ret_w_out[j]), **post)
        else:
            uv = mm_in(sgu_w_in, j, sgu_w_in.shape[2], kind="gelu", out_dtype=BF16)
            o = _sgu_core(uv, sgu_ln_g[j], sgu_ln_b[j], sgu_w_s[j], sgu_b_s[j], seq=seq)
            xf, h = _mm_out(o, bf(sgu_w_out[j]), **post)
        u = _mm_in(h, ffn_w_in, i, ffn_w_in.shape[2] // 2, seq=seq, kind="convffn",
                   out_dtype=BF16, conv_w=ffn_conv_w, conv_b=ffn_conv_b)
        nxt = (i + 1, 0, norm_g[i + 1, 0]) if i + 1 < depth else None
        res = _mm_out(u, bf(ffn_w_out[i]), xf, mod5, i, 1, norm_g[i, 3], seq=seq, nxt=nxt)
        xf, h = res if nxt is not None else (res, None)
    return xf.reshape(bsz, seq, d)
```

```python
import functools
import math

import jax
import jax.numpy as jnp
from jax import lax
from jax.experimental import pallas as pl
from jax.experimental.pallas import tpu as pltpu

F32 = jnp.float32
BF16 = jnp.bfloat16

EPS = 1e-6
CHUNK = 64
N_MIXERS = 4
N_MOD = 6
GLA_HEADS = 4
GLA_GATE_TEMP = 16.0
RET_HEADS = 8
ROPE_BASE = 10000.0
SGU_CHUNK = 128
SGU_HEADS = 8

LANE = 128
BF16_ROWS = 16
CONV_HALO = 32
VMEM_LIMIT = 60 * 1024 * 1024


def _cparams(n_axes):
    return pltpu.CompilerParams(
        dimension_semantics=("arbitrary",) * n_axes,
        vmem_limit_bytes=VMEM_LIMIT)


def _sigmoid(x):
    return 1.0 / (1.0 + jnp.exp(-x))


def _silu(x):
    return x * _sigmoid(x)


def _dot(a, b):
    return jnp.dot(a, b, preferred_element_type=F32)


def _dot_nt(a, b):
    return lax.dot_general(a, b, (((1,), (1,)), ((), ())), preferred_element_type=F32)


def _dot_tn(a, b):
    return lax.dot_general(a, b, (((0,), (0,)), ((), ())), preferred_element_type=F32)


def _pick(n, candidates):
    for c in candidates:
        if n % c == 0:
            return c
    raise ValueError(f"no tile for {n} in {candidates}")


def _ada_kernel(c_ref, w_ref, b_ref, o_ref):
    ca = _silu(c_ref[...]).astype(BF16)
    o_ref[...] = _dot(ca, w_ref[...].astype(BF16)) + b_ref[...]


def _ada(c, ada_w, ada_b):
    depth, d, n = ada_w.shape
    bsz = c.shape[0]
    tn = _pick(n, (1536, 1024, 512, 256, 128))
    return pl.pallas_call(
        _ada_kernel,
        grid=(depth, n // tn),
        in_specs=[
            pl.BlockSpec((bsz, d), lambda l, j: (0, 0)),
            pl.BlockSpec((None, d, tn), lambda l, j: (l, 0, j)),
            pl.BlockSpec((None, 1, tn), lambda l, j: (l, 0, j)),
        ],
        out_specs=pl.BlockSpec((None, bsz, tn), lambda l, j: (l, 0, j)),
        out_shape=jax.ShapeDtypeStruct((depth, bsz, n), F32),
        compiler_params=_cparams(2),
        name="ada_proj",
    )(c, ada_w, ada_b.reshape(depth, 1, n))


def _modnorm(x, gs, shift):
    ms = jnp.mean(x * x, axis=-1, keepdims=True)
    return (x * lax.rsqrt(ms + EPS) * gs + shift).astype(BF16)


def _modnorm_kernel(x_ref, mod_ref, g_ref, h_ref, *, tm):
    gs = g_ref[...] * (1.0 + mod_ref[1:2, :])
    shift = mod_ref[0:1, :]
    rp = BF16_ROWS
    for r in range(tm // rp):
        rows = slice(r * rp, (r + 1) * rp)
        h_ref[rows, :] = _modnorm(x_ref[rows, :], gs, shift)


def _modnorm_call(x, mod5, layer, slot, g, *, seq, tm=None):
    t, d = x.shape
    tm = tm or _pick(seq, (512, 256, 128))
    tps = seq // tm
    return pl.pallas_call(
        functools.partial(_modnorm_kernel, tm=tm),
        grid=(t // tm,),
        in_specs=[
            pl.BlockSpec((tm, d), lambda i: (i, 0)),
            pl.BlockSpec((None, None, None, 3, d), lambda i: (layer, i // tps, slot, 0, 0)),
            pl.BlockSpec((1, d), lambda i: (0, 0)),
        ],
        out_specs=pl.BlockSpec((tm, d), lambda i: (i, 0)),
        out_shape=jax.ShapeDtypeStruct((t, d), BF16),
        compiler_params=_cparams(1), name="modnorm",
    )(x, mod5, g.reshape(1, d))


def _cast_w(w_ref, wb_ref):
    step = 256
    for r in range(w_ref.shape[0] // step):
        rows = slice(r * step, (r + 1) * step)
        wb_ref[rows, :] = w_ref[rows, :].astype(BF16)


def _mm_in_kernel(h_ref, w_ref, o_ref, wb_ref, *, tm, rc, act, silu_from):
    j = pl.program_id(0)

    @pl.when(pl.program_id(1) == 0)
    def _():
        _cast_w(w_ref, wb_ref)

    def body(gate):
        for r in range(tm // rc):
            rows = slice(r * rc, (r + 1) * rc)
            y = _dot(h_ref[rows, :], wb_ref[...])
            if act == "gelu":
                y = 0.5 * y * (1.0 + lax.erf(y * (1.0 / math.sqrt(2.0))))
            if gate:
                y = _silu(y)
            o_ref[rows, :] = y.astype(o_ref.dtype)

    if silu_from is None:
        body(False)
    else:
        pl.when(j < silu_from)(lambda: body(False))
        pl.when(j >= silu_from)(lambda: body(True))


def _mm_glu_kernel(h_ref, wa_ref, wg_ref, o_ref, wab_ref, wgb_ref, *, tm, rc):
    @pl.when(pl.program_id(1) == 0)
    def _():
        _cast_w(wa_ref, wab_ref)
        _cast_w(wg_ref, wgb_ref)

    for r in range(tm // rc):
        rows = slice(r * rc, (r + 1) * rc)
        a = _dot(h_ref[rows, :], wab_ref[...])
        gate = _dot(h_ref[rows, :], wgb_ref[...])
        o_ref[rows, :] = (a * _sigmoid(gate)).astype(o_ref.dtype)


def _mm_convffn_kernel(h_ref, wa_ref, wg_ref, cw_ref, cb_ref, o_ref, wab_ref, wgb_ref, tail_ref,
                       *, tm, rc, tiles_per_seq):
    hr = 8
    i = pl.program_id(1)

    @pl.when(i == 0)
    def _():
        _cast_w(wa_ref, wab_ref)
        _cast_w(wg_ref, wgb_ref)

    @pl.when(i % tiles_per_seq == 0)
    def _():
        tail_ref[...] = jnp.zeros_like(tail_ref)

    w0 = cw_ref[0:1, :]
    w1 = cw_ref[1:2, :]
    w2 = cw_ref[2:3, :]
    cb = cb_ref[...]
    tail = tail_ref[...]
    for r in range(tm // rc):
        rows = slice(r * rc, (r + 1) * rc)
        a = _dot(h_ref[rows, :], wab_ref[...])
        b = _dot(h_ref[rows, :], wgb_ref[...])
        af = jnp.concatenate([tail, a], axis=0)
        a1 = pltpu.roll(af, 1, 0)[hr:, :]
        a2 = pltpu.roll(af, 2, 0)[hr:, :]
        conv = a2 * w0 + a1 * w1 + a * w2 + cb
        o_ref[rows, :] = (_silu(conv) * b).astype(o_ref.dtype)
        tail = a[rc - hr:, :]
    tail_ref[...] = tail


def _mm_in(h, w, widx, n_out, *, seq, kind, out_dtype, conv_w=None, conv_b=None,
           silu_cols=None, tm=None, tn=None, rc=256):
    t, d = h.shape
    tm = tm or _pick(seq, (1024, 512, 256, 128) if kind == "convffn" else (2048, 1024, 512, 256, 128))
    tn = tn or _pick(n_out, (512, 256, 128) if kind in ("glu", "convffn") else (1024, 512, 256, 128))
    rc = min(rc, tm)
    tps = seq // tm
    nb = n_out // tn
    grid = (nb, t // tm)
    h_spec = pl.BlockSpec((tm, d), lambda j, i: (i, 0))
    wa_spec = pl.BlockSpec((None, d, tn), lambda j, i: (widx, 0, j))
    wg_spec = pl.BlockSpec((None, d, tn), lambda j, i: (widx, 0, j + nb))
    out_spec = pl.BlockSpec((tm, tn), lambda j, i: (i, j))
    out_shape = jax.ShapeDtypeStruct((t, n_out), out_dtype)
    w_scratch = pltpu.VMEM((d, tn), BF16)
    if kind in ("plain", "gelu"):
        silu_from = None
        if silu_cols is not None:
            assert silu_cols % tn == 0
            silu_from = silu_cols // tn
        return pl.pallas_call(
            functools.partial(_mm_in_kernel, tm=tm, rc=rc, act=kind, silu_from=silu_from),
            grid=grid,
            in_specs=[h_spec, wa_spec],
            out_specs=out_spec, out_shape=out_shape,
            scratch_shapes=[w_scratch],
            compiler_params=_cparams(2), name=f"mm_in_{kind}",
        )(h, w)
    if kind == "glu":
        return pl.pallas_call(
            functools.partial(_mm_glu_kernel, tm=tm, rc=rc),
            grid=grid,
            in_specs=[h_spec, wa_spec, wg_spec],
            out_specs=out_spec, out_shape=out_shape,
            scratch_shapes=[w_scratch, w_scratch],
            compiler_params=_cparams(2), name="mm_in_glu",
        )(h, w, w)
    assert kind == "convffn"
    kw = conv_w.shape[1]
    assert kw == 3
    return pl.pallas_call(
        functools.partial(_mm_convffn_kernel, tm=tm, rc=rc, tiles_per_seq=tps),
        grid=grid,
        in_specs=[h_spec, wa_spec, wg_spec,
                  pl.BlockSpec((None, kw, tn), lambda j, i: (widx, 0, j)),
                  pl.BlockSpec((None, 1, tn), lambda j, i: (widx, 0, j))],
        out_specs=out_spec, out_shape=out_shape,
        scratch_shapes=[w_scratch, w_scratch, pltpu.VMEM((8, tn), F32)],
        compiler_params=_cparams(2), name="mm_in_convffn",
    )(h, w, w, conv_w, conv_b.reshape(conv_b.shape[0], 1, n_out))


def _post(y, x, gg):
    ms = jnp.mean(y * y, axis=-1, keepdims=True)
    return x + y * lax.rsqrt(ms + EPS) * gg


def _next_specs(nxt, mod5, d, tm, tps, t):
    if nxt is None:
        return [], [], [], []
    layer_n, slot_n, g_n = nxt
    in_specs = [pl.BlockSpec((None, None, None, 3, d), lambda i: (layer_n, i // tps, slot_n, 0, 0)),
                pl.BlockSpec((1, d), lambda i: (0, 0))]
    return (in_specs, [mod5, g_n.reshape(1, d)],
            [pl.BlockSpec((tm, d), lambda i: (i, 0))], [jax.ShapeDtypeStruct((t, d), BF16)])


def _mm_out_kernel(u_ref, w_ref, x_ref, mod_ref, g_ref, *rest, tm, rc, has_next):
    if has_next:
        modn_ref, gn_ref, o_ref, hn_ref = rest
        gs = gn_ref[...] * (1.0 + modn_ref[1:2, :])
        shift = modn_ref[0:1, :]
    else:
        (o_ref,) = rest
    gg = mod_ref[2:3, :] * g_ref[...]
    for r in range(tm // rc):
        rows = slice(r * rc, (r + 1) * rc)
        y = _dot(u_ref[rows, :], w_ref[...])
        xn = _post(y, x_ref[rows, :], gg)
        o_ref[rows, :] = xn
        if has_next:
            hn_ref[rows, :] = _modnorm(xn, gs, shift)


def _mm_out(u, w, x, mod5, layer, slot, g, *, seq, nxt=None, tm=None, rc=128):
    t, k = u.shape
    d = w.shape[1]
    tm = tm or _pick(seq, (512, 256, 128))
    rc = min(rc, tm)
    tps = seq // tm
    n_in, n_args, n_out, n_shape = _next_specs(nxt, mod5, d, tm, tps, t)
    res = pl.pallas_call(
        functools.partial(_mm_out_kernel, tm=tm, rc=rc, has_next=nxt is not None),
        grid=(t // tm,),
        in_specs=[
            pl.BlockSpec((tm, k), lambda i: (i, 0)),
            pl.BlockSpec((k, d), lambda i: (0, 0), pipeline_mode=pl.Buffered(1)),
            pl.BlockSpec((tm, d), lambda i: (i, 0)),
            pl.BlockSpec((None, None, None, 3, d), lambda i: (layer, i // tps, slot, 0, 0)),
            pl.BlockSpec((1, d), lambda i: (0, 0)),
        ] + n_in,
        out_specs=[pl.BlockSpec((tm, d), lambda i: (i, 0))] + n_out,
        out_shape=[jax.ShapeDtypeStruct((t, d), F32)] + n_shape,
        compiler_params=_cparams(1), name="mm_out",
    )(u, w, x, mod5, g.reshape(1, d), *n_args)
    return res if nxt is not None else res[0]


def _conv_out_kernel(z_ref, zh_ref, dw_ref, db_ref, lg_ref, lb_ref, w_ref, x_ref, mod_ref, g_ref,
                     modn_ref, gn_ref, o_ref, hn_ref, zf_ref, zc_ref, hb_ref,
                     *, tm, kw, rc, cc, rl, tiles_per_seq):
    d = z_ref.shape[1]
    halo = CONV_HALO
    sub = 8
    at_start = pl.program_id(0) % tiles_per_seq == 0
    zf_ref[0:halo, :] = jnp.where(at_start, 0.0, zh_ref[...])
    zf_ref[halo:, :] = z_ref[...]
    base = halo - (kw - 1)
    n = rc + halo
    for r in range(tm // rc):
        for c in range(d // cc):
            cols = slice(c * cc, (c + 1) * cc)
            blk = zf_ref[r * rc:r * rc + n, cols]
            acc = jnp.broadcast_to(db_ref[:, cols], (rc, cc))
            for s in range(sub):
                sh = blk if s == 0 else pltpu.roll(blk, n - s, 0)
                for q in range(halo // sub + 1):
                    k = sub * q + s - base
                    if 0 <= k < kw:
                        acc = acc + sh[sub * q:sub * q + rc, :] * dw_ref[k:k + 1, cols]
            zc_ref[r * rc:(r + 1) * rc, cols] = acc
    lg = lg_ref[...]
    lb = lb_ref[...]
    for r in range(tm // rl):
        rows = slice(r * rl, (r + 1) * rl)
        zc = zc_ref[rows, :]
        mu = jnp.mean(zc, axis=-1, keepdims=True)
        zc = zc - mu
        var = jnp.mean(zc * zc, axis=-1, keepdims=True)
        y = zc * lax.rsqrt(var + EPS) * lg + lb
        hb_ref[rows, :] = _silu(y).astype(BF16)
    gg = mod_ref[2:3, :] * g_ref[...]
    gs = gn_ref[...] * (1.0 + modn_ref[1:2, :])
    shift = modn_ref[0:1, :]
    ro = 128
    for r in range(tm // ro):
        rows = slice(r * ro, (r + 1) * ro)
        y = _dot(hb_ref[rows, :], w_ref[...])
        xn = _post(y, x_ref[rows, :], gg)
        o_ref[rows, :] = xn
        hn_ref[rows, :] = _modnorm(xn, gs, shift)


def _conv_out(z, dw_w, dw_b, ln_g, ln_b, w, x, mod5, layer, slot, g, *, seq, nxt, tm=None):
    t, d = z.shape
    kw = dw_w.shape[0]
    assert kw - 1 <= CONV_HALO
    tm = tm or _pick(seq, (256, 128))
    tps = seq // tm
    hb = tm // CONV_HALO
    full = lambda i: (0, 0)
    n_in, n_args, n_out, n_shape = _next_specs(nxt, mod5, d, tm, tps, t)
    return pl.pallas_call(
        functools.partial(_conv_out_kernel, tm=tm, kw=kw, rc=tm, cc=LANE, rl=32,
                          tiles_per_seq=tps),
        grid=(t // tm,),
        in_specs=[
            pl.BlockSpec((tm, d), lambda i: (i, 0)),
            pl.BlockSpec((CONV_HALO, d), lambda i: (jnp.maximum(i * hb - 1, 0), 0)),
            pl.BlockSpec((kw, d), full),
            pl.BlockSpec((1, d), full),
            pl.BlockSpec((1, d), full),
            pl.BlockSpec((1, d), full),
            pl.BlockSpec((d, d), full, pipeline_mode=pl.Buffered(1)),
            pl.BlockSpec((tm, d), lambda i: (i, 0)),
            pl.BlockSpec((None, None, None, 3, d), lambda i: (layer, i // tps, slot, 0, 0)),
            pl.BlockSpec((1, d), full),
        ] + n_in,
        out_specs=[pl.BlockSpec((tm, d), lambda i: (i, 0))] + n_out,
        out_shape=[jax.ShapeDtypeStruct((t, d), F32)] + n_shape,
        scratch_shapes=[pltpu.VMEM((tm + CONV_HALO, d), F32), pltpu.VMEM((tm, d), F32),
                        pltpu.VMEM((tm, d), BF16)],
        compiler_params=_cparams(1), name="conv_out",
    )(z, z, dw_w, dw_b.reshape(1, d), ln_g.reshape(1, d), ln_b.reshape(1, d), w, x, mod5,
      g.reshape(1, d), *n_args)


def _split3(a):
    hi = a.astype(BF16)
    r1 = a - hi.astype(F32)
    mid = r1.astype(BF16)
    lo = (r1 - mid.astype(F32)).astype(BF16)
    return hi, mid, lo


def _gla_kernel(q_ref, k_ref, v_ref, r_ref, gl_ref, wg_ref, bg_ref, ng_ref, o_ref, st_ref,
                *, heads, dk, dv, tm):
    @pl.when(pl.program_id(1) == 0)
    def _():
        st_ref[...] = jnp.zeros_like(st_ref)

    hk = heads * dk
    ti = lax.broadcasted_iota(jnp.int32, (CHUNK, CHUNK), 0)
    si = lax.broadcasted_iota(jnp.int32, (CHUNK, CHUNK), 1)
    tril = jnp.where(si <= ti, 1.0, 0.0).astype(BF16)
    t2 = lax.broadcasted_iota(jnp.int32, (CHUNK, 2 * CHUNK), 0)
    l2 = lax.broadcasted_iota(jnp.int32, (CHUNK, 2 * CHUNK), 1)
    keep_fwd = l2 <= t2
    keep_bwd = l2 - CHUNK > t2
    scale = dk ** -0.5
    ng = ng_ref[...]
    for c in range(tm // CHUNK):
        rows = slice(c * CHUNK, (c + 1) * CHUNK)
        zg = _dot(gl_ref[rows, :], wg_ref[...]) + bg_ref[...]
        log_a = (jnp.minimum(zg, 0.0) - jnp.log1p(jnp.exp(-jnp.abs(zg)))) * (1.0 / GLA_GATE_TEMP)
        parts = _dot(tril, jnp.concatenate(_split3(log_a), axis=1))
        b = parts[:, :hk] + parts[:, hk:2 * hk] + parts[:, 2 * hk:]
        b_last = b[CHUNK - 1:CHUNK, :]
        e_pos = jnp.exp(b)
        e_neg = jnp.exp(-b)
        q = q_ref[rows, :].astype(F32) * scale
        k = k_ref[rows, :].astype(F32)
        q_fwd = (q * e_pos).astype(BF16)
        k_fwd = (k * e_neg).astype(BF16)
        q_bwd = (q * e_neg).astype(BF16)
        k_bwd = (k * e_pos).astype(BF16)
        k_state = (k * jnp.exp(b_last - b)).astype(BF16)
        decay = jnp.transpose(jnp.broadcast_to(jnp.exp(b_last), (LANE, hk)))
        for h in range(heads):
            kc = slice(h * dk, (h + 1) * dk)
            vc = slice(h * dv, (h + 1) * dv)
            v = v_ref[rows, vc]
            s2 = _dot_nt(jnp.concatenate([q_fwd[:, kc], q_bwd[:, kc]], axis=0),
                         jnp.concatenate([k_fwd[:, kc], k_bwd[:, kc]], axis=0))
            scores = jnp.where(keep_fwd, s2[:CHUNK, :], jnp.where(keep_bwd, s2[CHUNK:, :], 0.0))
            st = st_ref[h]
            o = _dot(jnp.concatenate([q_fwd[:, kc], scores.astype(BF16)], axis=1),
                     jnp.concatenate([st.astype(BF16), v, v], axis=0))
            st_ref[h] = (st * jnp.concatenate([decay[kc, :]] * (dv // LANE), axis=1)
                         + _dot_tn(k_state[:, kc], v))
            ms = jnp.mean(o * o, axis=-1, keepdims=True)
            o = o * lax.rsqrt(ms + EPS) * ng
            o_ref[rows, vc] = (o * r_ref[rows, vc].astype(F32)).astype(o_ref.dtype)


def _gla_core(qkvr, glow, w_gate, b_gate, norm_g, *, bsz, seq, tm=None):
    heads = GLA_HEADS
    hk = w_gate.shape[1]
    dk = hk // heads
    dv = norm_g.shape[0]
    hv = heads * dv
    assert qkvr.shape[1] == 2 * hk + 2 * hv and hv % hk == 0
    tm = tm or _pick(seq, (256, 128, 64))
    tps = seq // tm
    row = lambda b, t: b * tps + t
    return pl.pallas_call(
        functools.partial(_gla_kernel, heads=heads, dk=dk, dv=dv, tm=tm),
        grid=(bsz, tps),
        in_specs=[
            pl.BlockSpec((tm, hk), lambda b, t: (row(b, t), 0)),
            pl.BlockSpec((tm, hk), lambda b, t: (row(b, t), 1)),
            pl.BlockSpec((tm, hv), lambda b, t: (row(b, t), (2 * hk) // hv)),
            pl.BlockSpec((tm, hv), lambda b, t: (row(b, t), (2 * hk) // hv + 1)),
            pl.BlockSpec((tm, LANE), lambda b, t: (row(b, t), 0)),
            pl.BlockSpec((LANE, hk), lambda b, t: (0, 0)),
            pl.BlockSpec((1, hk), lambda b, t: (0, 0)),
            pl.BlockSpec((1, dv), lambda b, t: (0, 0)),
        ],
        out_specs=pl.BlockSpec((tm, hv), lambda b, t: (row(b, t), 0)),
        out_shape=jax.ShapeDtypeStruct((bsz * seq, hv), BF16),
        scratch_shapes=[pltpu.VMEM((heads, dk, dv), F32)],
        compiler_params=_cparams(2), name="gla_core",
    )(qkvr, qkvr, qkvr, qkvr, glow, w_gate, b_gate.reshape(1, hk), norm_g.reshape(1, dv))


def _ret_kernel(q_ref, k_ref, v_ref, g_ref, cos_ref, sin_ref, o_ref, st_ref, *, heads, dk, dv, tm):
    @pl.when(pl.program_id(1) == 0)
    def _():
        st_ref[...] = jnp.zeros_like(st_ref)

    half = dk // 2
    ti = lax.broadcasted_iota(jnp.int32, (tm, tm), 0)
    si = lax.broadcasted_iota(jnp.int32, (tm, tm), 1)
    visible = si // CHUNK <= ti // CHUNK
    dist = jnp.abs((ti - si).astype(F32))
    pos = lax.broadcasted_iota(jnp.int32, (tm, LANE), 0).astype(F32)
    kscale = dk ** -0.5
    cos = cos_ref[...]
    sin = sin_ref[...]

    def rot(t):
        t1 = t[:, :half]
        t2 = t[:, half:]
        return jnp.concatenate([t1 * cos - t2 * sin, t1 * sin + t2 * cos], axis=1)

    for h in range(heads):
        log_g = math.log1p(-(2.0 ** (-5.0 - h)))
        weight = jnp.where(visible, jnp.exp(log_g * dist), 0.0)
        q_dec = jnp.concatenate([jnp.exp(log_g * (pos + 1.0))] * (dk // LANE), axis=1)
        k_dec = jnp.concatenate([jnp.exp(log_g * (tm - 1.0 - pos))] * (dk // LANE), axis=1)
        tile_decay = math.exp(log_g * tm)
        kc = slice(h * dk, (h + 1) * dk)
        vc = slice(h * dv, (h + 1) * dv)
        q = rot(q_ref[:, kc].astype(F32))
        k = rot(k_ref[:, kc].astype(F32)) * kscale
        v = v_ref[:, vc]
        scores = _dot_nt(q.astype(BF16), k.astype(BF16)) * weight
        st = st_ref[h]
        o = _dot(jnp.concatenate([scores.astype(BF16), (q * q_dec).astype(BF16)], axis=1),
                 jnp.concatenate([v, st.astype(BF16)], axis=0))
        st_ref[h] = st * tile_decay + _dot_tn((k * k_dec).astype(BF16), v)
        ms = jnp.mean(o * o, axis=-1, keepdims=True)
        o = o * lax.rsqrt(ms + EPS)
        o_ref[:, vc] = (o * g_ref[:, vc].astype(F32)).astype(o_ref.dtype)


def _ret_core(qkvg, *, bsz, seq, tm=None):
    heads = RET_HEADS
    n = qkvg.shape[1]
    hk = n // 6
    hv = 2 * hk
    dk = hk // heads
    dv = hv // heads
    half = dk // 2
    tm = tm or _pick(seq, (256, 128, 64))
    tps = seq // tm
    pos = jnp.arange(seq)
    inv_freq = ROPE_BASE ** (-jnp.arange(half, dtype=F32) / half)
    ang = pos.astype(F32)[:, None] * inv_freq
    cos = jnp.cos(ang)
    sin = jnp.sin(ang)
    row = lambda b, t: b * tps + t
    return pl.pallas_call(
        functools.partial(_ret_kernel, heads=heads, dk=dk, dv=dv, tm=tm),
        grid=(bsz, tps),
        in_specs=[
            pl.BlockSpec((tm, hk), lambda b, t: (row(b, t), 0)),
            pl.BlockSpec((tm, hk), lambda b, t: (row(b, t), 1)),
            pl.BlockSpec((tm, hv), lambda b, t: (row(b, t), 1)),
            pl.BlockSpec((tm, hv), lambda b, t: (row(b, t), 2)),
            pl.BlockSpec((tm, half), lambda b, t: (t, 0)),
            pl.BlockSpec((tm, half), lambda b, t: (t, 0)),
        ],
        out_specs=pl.BlockSpec((tm, hv), lambda b, t: (row(b, t), 0)),
        out_shape=jax.ShapeDtypeStruct((bsz * seq, hv), BF16),
        scratch_shapes=[pltpu.VMEM((heads, dk, dv), F32)],
        compiler_params=_cparams(2), name="ret_core",
    )(qkvg, qkvg, qkvg, qkvg, cos, sin)


def _sgu_kernel(u_ref, v_ref, lg_ref, lb_ref, ws_ref, bs_ref, o_ref, vn_ref, *, heads, hd, tm):
    lg = lg_ref[...]
    lb = lb_ref[...]
    rc = 64
    for r in range(tm // rc):
        rows = slice(r * rc, (r + 1) * rc)
        v = v_ref[rows, :].astype(F32)
        mu = jnp.mean(v, axis=-1, keepdims=True)
        v = v - mu
        var = jnp.mean(v * v, axis=-1, keepdims=True)
        vn_ref[rows, :] = (v * lax.rsqrt(var + EPS) * lg + lb).astype(BF16)
    pi = lax.broadcasted_iota(jnp.int32, (SGU_CHUNK, SGU_CHUNK), 0)
    pj = lax.broadcasted_iota(jnp.int32, (SGU_CHUNK, SGU_CHUNK), 1)
    mask = (pj // CHUNK) <= (pi // CHUNK)
    for h in range(heads):
        w = jnp.where(mask, ws_ref[h], 0.0).astype(BF16)
        bias = bs_ref[h]
        cols = slice(h * hd, (h + 1) * hd)
        for n in range(tm // SGU_CHUNK):
            rows = slice(n * SGU_CHUNK, (n + 1) * SGU_CHUNK)
            sv = _dot(w, vn_ref[rows, cols]) + bias
            o_ref[rows, cols] = (u_ref[rows, cols].astype(F32) * sv).astype(o_ref.dtype)


def _sgu_core(uv, ln_g, ln_b, w_s, b_s, *, seq, tm=None):
    t = uv.shape[0]
    width = uv.shape[1] // 2
    heads = SGU_HEADS
    hd = width // heads
    tm = tm or _pick(seq, (256, 128))
    full2 = lambda i: (0, 0)
    full3 = lambda i: (0, 0, 0)
    return pl.pallas_call(
        functools.partial(_sgu_kernel, heads=heads, hd=hd, tm=tm),
        grid=(t // tm,),
        in_specs=[
            pl.BlockSpec((tm, width), lambda i: (i, 0)),
            pl.BlockSpec((tm, width), lambda i: (i, 1)),
            pl.BlockSpec((1, width), full2),
            pl.BlockSpec((1, width), full2),
            pl.BlockSpec((heads, SGU_CHUNK, SGU_CHUNK), full3),
            pl.BlockSpec((heads, SGU_CHUNK, 1), full3),
        ],
        out_specs=pl.BlockSpec((tm, width), lambda i: (i, 0)),
        out_shape=jax.ShapeDtypeStruct((t, width), BF16),
        scratch_shapes=[pltpu.VMEM((tm, width), BF16)],
        compiler_params=_cparams(1), name="sgu_core",
    )(uv, uv, ln_g.reshape(1, width), ln_b.reshape(1, width), w_s,
      b_s.reshape(heads, SGU_CHUNK, 1))


def kernel(x, c, ada_w, ada_b, norm_g, ffn_w_in, ffn_conv_w, ffn_conv_b, ffn_w_out, cm_w_in, cm_dw_w, cm_dw_b, cm_ln_g, cm_ln_b, cm_w_out, gla_w_in, gla_w_gate, gla_b_gate, gla_norm_g, gla_w_out, ret_w_in, ret_w_out, sgu_w_in, sgu_ln_g, sgu_ln_b, sgu_w_s, sgu_b_s, sgu_w_out):
    bsz, seq, d = x.shape
    depth = ada_w.shape[0]
    mod5 = _ada(c, ada_w, ada_b).reshape(depth, bsz, 2, 3, d)
    xf = x.reshape(bsz * seq, d)
    bf = lambda a: a.astype(BF16)
    h = _modnorm_call(xf, mod5, 0, 0, norm_g[0, 0], seq=seq)
    for i in range(depth):
        m, j = i % N_MIXERS, i // N_MIXERS
        mm_in = functools.partial(_mm_in, h, seq=seq)
        post = dict(x=xf, mod5=mod5, layer=i, slot=0, g=norm_g[i, 1], seq=seq,
                    nxt=(i, 1, norm_g[i, 2]))
        if m == 0:
            z = mm_in(cm_w_in, j, cm_w_in.shape[2] // 2, kind="glu", out_dtype=F32)
            xf, h = _conv_out(z, cm_dw_w[j], cm_dw_b[j], cm_ln_g[j], cm_ln_b[j], bf(cm_w_out[j]), **post)
        elif m == 1:
            rank, hk = gla_w_gate.shape[1:]
            n_main = gla_w_in.shape[2] - rank
            hv = (n_main - 2 * hk) // 2
            w_low = jnp.pad(gla_w_in[j][:, n_main:], ((0, 0), (0, LANE - rank)))
            w_gate = jnp.pad(gla_w_gate[j], ((0, LANE - rank), (0, 0)))
            qkvr = mm_in(gla_w_in, j, n_main, kind="plain", out_dtype=BF16, silu_cols=2 * hk + hv)
            glow = mm_in(w_low[None], 0, LANE, kind="plain", out_dtype=BF16)
            o = _gla_core(qkvr, glow, bf(w_gate), gla_b_gate[j], gla_norm_g[j], bsz=bsz, seq=seq)
            xf, h = _mm_out(o, bf(gla_w_out[j]), **post)
        elif m == 2:
            n_ret = ret_w_in.shape[2]
            qkvg = mm_in(ret_w_in, j, n_ret, kind="plain", out_dtype=BF16, silu_cols=n_ret - n_ret // 3)
            o = _ret_core(qkvg, bsz=bsz, seq=seq)
            xf, h = _mm_out(o, bf(ret_w_out[j]), **post)
        else:
            uv = mm_in(sgu_w_in, j, sgu_w_in.shape[2], kind="gelu", out_dtype=BF16)
            o = _sgu_core(uv, sgu_ln_g[j], sgu_ln_b[j], sgu_w_s[j], sgu_b_s[j], seq=seq)
            xf, h = _mm_out(o, bf(sgu_w_out[j]), **post)
        u = _mm_in(h, ffn_w_in, i, ffn_w_in.shape[2] // 2, seq=seq, kind="convffn",
                   out_dtype=BF16, conv_w=ffn_conv_w, conv_b=ffn_conv_b)
        nxt = (i + 1, 0, norm_g[i + 1, 0]) if i + 1 < depth else None
        res = _mm_out(u, bf(ffn_w_out[i]), xf, mod5, i, 1, norm_g[i, 3], seq=seq, nxt=nxt)
        xf, h = res if nxt is not None else (res, None)
    return xf.reshape(bsz, seq, d)
```

```python
import functools
import math

import jax
import jax.numpy as jnp
from jax import lax
from jax.experimental import pallas as pl
from jax.experimental.pallas import tpu as pltpu

F32 = jnp.float32
BF16 = jnp.bfloat16

EPS = 1e-6
CHUNK = 64
N_MIXERS = 4
N_MOD = 6
GLA_HEADS = 4
GLA_GATE_TEMP = 16.0
RET_HEADS = 8
ROPE_BASE = 10000.0
SGU_CHUNK = 128
SGU_HEADS = 8

LANE = 128
BF16_ROWS = 16
CONV_HALO = 32
VMEM_LIMIT = 60 * 1024 * 1024


def _cparams(n_axes):
    return pltpu.CompilerParams(
        dimension_semantics=("arbitrary",) * n_axes,
        vmem_limit_bytes=VMEM_LIMIT)


def _sigmoid(x):
    return 1.0 / (1.0 + jnp.exp(-x))


def _silu(x):
    return x * _sigmoid(x)


def _dot(a, b):
    return jnp.dot(a, b, preferred_element_type=F32)


def _dot_nt(a, b):
    return lax.dot_general(a, b, (((1,), (1,)), ((), ())), preferred_element_type=F32)


def _dot_tn(a, b):
    return lax.dot_general(a, b, (((0,), (0,)), ((), ())), preferred_element_type=F32)


def _pick(n, candidates):
    for c in candidates:
        if n % c == 0:
            return c
    raise ValueError(f"no tile for {n} in {candidates}")


def _ada_kernel(c_ref, w_ref, b_ref, o_ref):
    ca = _silu(c_ref[...]).astype(BF16)
    o_ref[...] = _dot(ca, w_ref[...].astype(BF16)) + b_ref[...]


def _ada(c, ada_w, ada_b):
    depth, d, n = ada_w.shape
    bsz = c.shape[0]
    tn = _pick(n, (1536, 1024, 512, 256, 128))
    return pl.pallas_call(
        _ada_kernel,
        grid=(depth, n // tn),
        in_specs=[
            pl.BlockSpec((bsz, d), lambda l, j: (0, 0)),
            pl.BlockSpec((None, d, tn), lambda l, j: (l, 0, j)),
            pl.BlockSpec((None, 1, tn), lambda l, j: (l, 0, j)),
        ],
        out_specs=pl.BlockSpec((None, bsz, tn), lambda l, j: (l, 0, j)),
        out_shape=jax.ShapeDtypeStruct((depth, bsz, n), F32),
        compiler_params=_cparams(2),
        name="ada_proj",
    )(c, ada_w, ada_b.reshape(depth, 1, n))


def _modnorm(x, gs, shift):
    ms = jnp.mean(x * x, axis=-1, keepdims=True)
    return (x * lax.rsqrt(ms + EPS) * gs + shift).astype(BF16)


def _modnorm_kernel(x_ref, mod_ref, g_ref, h_ref, *, tm):
    gs = g_ref[...] * (1.0 + mod_ref[1:2, :])
    shift = mod_ref[0:1, :]
    rp = BF16_ROWS
    for r in range(tm // rp):
        rows = slice(r * rp, (r + 1) * rp)
        h_ref[rows, :] = _modnorm(x_ref[rows, :], gs, shift)


def _modnorm_call(x, mod5, layer, slot, g, *, seq, tm=None):
    t, d = x.shape
    tm = tm or _pick(seq, (512, 256, 128))
    tps = seq // tm
    return pl.pallas_call(
        functools.partial(_modnorm_kernel, tm=tm),
        grid=(t // tm,),
        in_specs=[
            pl.BlockSpec((tm, d), lambda i: (i, 0)),
            pl.BlockSpec((None, None, None, 3, d), lambda i: (layer, i // tps, slot, 0, 0)),
            pl.BlockSpec((1, d), lambda i: (0, 0)),
        ],
        out_specs=pl.BlockSpec((tm, d), lambda i: (i, 0)),
        out_shape=jax.ShapeDtypeStruct((t, d), BF16),
        compiler_params=_cparams(1), name="modnorm",
    )(x, mod5, g.reshape(1, d))


def _cast_w(w_ref, wb_ref):
    step = 256
    for r in range(w_ref.shape[0] // step):
        rows = slice(r * step, (r + 1) * step)
        wb_ref[rows, :] = w_ref[rows, :].astype(BF16)


def _mm_in_kernel(h_ref, w_ref, o_ref, wb_ref, *, tm, rc, act, silu_from):
    j = pl.program_id(0)

    @pl.when(pl.program_id(1) == 0)
    def _():
        _cast_w(w_ref, wb_ref)

    def body(gate):
        for r in range(tm // rc):
            rows = slice(r * rc, (r + 1) * rc)
            y = _dot(h_ref[rows, :], wb_ref[...])
            if act == "gelu":
                y = 0.5 * y * (1.0 + lax.erf(y * (1.0 / math.sqrt(2.0))))
            if gate:
                y = _silu(y)
            o_ref[rows, :] = y.astype(o_ref.dtype)

    if silu_from is None:
        body(False)
    else:
        pl.when(j < silu_from)(lambda: body(False))
        pl.when(j >= silu_from)(lambda: body(True))


def _mm_glu_kernel(h_ref, wa_ref, wg_ref, o_ref, wab_ref, wgb_ref, *, tm, rc):
    @pl.when(pl.program_id(1) == 0)
    def _():
        _cast_w(wa_ref, wab_ref)
        _cast_w(wg_ref, wgb_ref)

    for r in range(tm // rc):
        rows = slice(r * rc, (r + 1) * rc)
        a = _dot(h_ref[rows, :], wab_ref[...])
        gate = _dot(h_ref[rows, :], wgb_ref[...])
        o_ref[rows, :] = (a * _sigmoid(gate)).astype(o_ref.dtype)


def _mm_convffn_kernel(h_ref, wa_ref, wg_ref, cw_ref, cb_ref, o_ref, wab_ref, wgb_ref, tail_ref,
                       *, tm, rc, tiles_per_seq):
    hr = 8
    i = pl.program_id(1)

    @pl.when(i == 0)
    def _():
        _cast_w(wa_ref, wab_ref)
        _cast_w(wg_ref, wgb_ref)

    @pl.when(i % tiles_per_seq == 0)
    def _():
        tail_ref[...] = jnp.zeros_like(tail_ref)

    w0 = cw_ref[0:1, :]
    w1 = cw_ref[1:2, :]
    w2 = cw_ref[2:3, :]
    cb = cb_ref[...]
    tail = tail_ref[...]
    for r in range(tm // rc):
        rows = slice(r * rc, (r + 1) * rc)
        a = _dot(h_ref[rows, :], wab_ref[...])
        b = _dot(h_ref[rows, :], wgb_ref[...])
        af = jnp.concatenate([tail, a], axis=0)
        a1 = pltpu.roll(af, 1, 0)[hr:, :]
        a2 = pltpu.roll(af, 2, 0)[hr:, :]
        conv = a2 * w0 + a1 * w1 + a * w2 + cb
        o_ref[rows, :] = (_silu(conv) * b).astype(o_ref.dtype)
        tail = a[rc - hr:, :]
    tail_ref[...] = tail


def _mm_in(h, w, widx, n_out, *, seq, kind, out_dtype, conv_w=None, conv_b=None,
           silu_cols=None, tm=None, tn=None, rc=256):
    t, d = h.shape
    tm = tm or _pick(seq, (1024, 512, 256, 128) if kind == "convffn" else (2048, 1024, 512, 256, 128))
    tn = tn or _pick(n_out, (512, 256, 128) if kind in ("glu", "convffn") else (1024, 512, 256, 128))
    rc = min(rc, tm)
    tps = seq // tm
    nb = n_out // tn
    grid = (nb, t // tm)
    h_spec = pl.BlockSpec((tm, d), lambda j, i: (i, 0))
    wa_spec = pl.BlockSpec((None, d, tn), lambda j, i: (widx, 0, j))
    wg_spec = pl.BlockSpec((None, d, tn), lambda j, i: (widx, 0, j + nb))
    out_spec = pl.BlockSpec((tm, tn), lambda j, i: (i, j))
    out_shape = jax.ShapeDtypeStruct((t, n_out), out_dtype)
    w_scratch = pltpu.VMEM((d, tn), BF16)
    if kind in ("plain", "gelu"):
        silu_from = None
        if silu_cols is not None:
            assert silu_cols % tn == 0
            silu_from = silu_cols // tn
        return pl.pallas_call(
            functools.partial(_mm_in_kernel, tm=tm, rc=rc, act=kind, silu_from=silu_from),
            grid=grid,
            in_specs=[h_spec, wa_spec],
            out_specs=out_spec, out_shape=out_shape,
            scratch_shapes=[w_scratch],
            compiler_params=_cparams(2), name=f"mm_in_{kind}",
        )(h, w)
    if kind == "glu":
        return pl.pallas_call(
            functools.partial(_mm_glu_kernel, tm=tm, rc=rc),
            grid=grid,
            in_specs=[h_spec, wa_spec, wg_spec],
            out_specs=out_spec, out_shape=out_shape,
            scratch_shapes=[w_scratch, w_scratch],
            compiler_params=_cparams(2), name="mm_in_glu",
        )(h, w, w)
    assert kind == "convffn"
    kw = conv_w.shape[1]
    assert kw == 3
    return pl.pallas_call(
        functools.partial(_mm_convffn_kernel, tm=tm, rc=rc, tiles_per_seq=tps),
        grid=grid,
        in_specs=[h_spec, wa_spec, wg_spec,
                  pl.BlockSpec((None, kw, tn), lambda j, i: (widx, 0, j)),
                  pl.BlockSpec((None, 1, tn), lambda j, i: (widx, 0, j))],
        out_specs=out_spec, out_shape=out_shape,
        scratch_shapes=[w_scratch, w_scratch, pltpu.VMEM((8, tn), F32)],
        compiler_params=_cparams(2), name="mm_in_convffn",
    )(h, w, w, conv_w, conv_b.reshape(conv_b.shape[0], 1, n_out))


def _post(y, x, gg):
    ms = jnp.mean(y * y, axis=-1, keepdims=True)
    return x + y * lax.rsqrt(ms + EPS) * gg


def _next_specs(nxt, mod5, d, tm, tps, t):
    if nxt is None:
        return [], [], [], []
    layer_n, slot_n, g_n = nxt
    in_specs = [pl.BlockSpec((None, None, None, 3, d), lambda i: (layer_n, i // tps, slot_n, 0, 0)),
                pl.BlockSpec((1, d), lambda i: (0, 0))]
    return (in_specs, [mod5, g_n.reshape(1, d)],
            [pl.BlockSpec((tm, d), lambda i: (i, 0))], [jax.ShapeDtypeStruct((t, d), BF16)])


def _mm_out_kernel(u_ref, w_ref, x_ref, mod_ref, g_ref, *rest, tm, rc, has_next):
    if has_next:
        modn_ref, gn_ref, o_ref, hn_ref = rest
        gs = gn_ref[...] * (1.0 + modn_ref[1:2, :])
        shift = modn_ref[0:1, :]
    else:
        (o_ref,) = rest
    gg = mod_ref[2:3, :] * g_ref[...]
    for r in range(tm // rc):
        rows = slice(r * rc, (r + 1) * rc)
        y = _dot(u_ref[rows, :], w_ref[...])
        xn = _post(y, x_ref[rows, :], gg)
        o_ref[rows, :] = xn
        if has_next:
            hn_ref[rows, :] = _modnorm(xn, gs, shift)


def _mm_out(u, w, x, mod5, layer, slot, g, *, seq, nxt=None, tm=None, rc=128):
    t, k = u.shape
    d = w.shape[1]
    tm = tm or _pick(seq, (512, 256, 128))
    rc = min(rc, tm)
    tps = seq // tm
    n_in, n_args, n_out, n_shape = _next_specs(nxt, mod5, d, tm, tps, t)
    res = pl.pallas_call(
        functools.partial(_mm_out_kernel, tm=tm, rc=rc, has_next=nxt is not None),
        grid=(t // tm,),
        in_specs=[
            pl.BlockSpec((tm, k), lambda i: (i, 0)),
            pl.BlockSpec((k, d), lambda i: (0, 0), pipeline_mode=pl.Buffered(1)),
            pl.BlockSpec((tm, d), lambda i: (i, 0)),
            pl.BlockSpec((None, None, None, 3, d), lambda i: (layer, i // tps, slot, 0, 0)),
            pl.BlockSpec((1, d), lambda i: (0, 0)),
        ] + n_in,
        out_specs=[pl.BlockSpec((tm, d), lambda i: (i, 0))] + n_out,
        out_shape=[jax.ShapeDtypeStruct((t, d), F32)] + n_shape,
        compiler_params=_cparams(1), name="mm_out",
    )(u, w, x, mod5, g.reshape(1, d), *n_args)
    return res if nxt is not None else res[0]


def _conv_out_kernel(z_ref, zh_ref, dw_ref, db_ref, lg_ref, lb_ref, w_ref, x_ref, mod_ref, g_ref,
                     modn_ref, gn_ref, o_ref, hn_ref, zf_ref, zc_ref, hb_ref,
                     *, tm, kw, rc, cc, rl, tiles_per_seq):
    d = z_ref.shape[1]
    halo = CONV_HALO
    sub = 8
    at_start = pl.program_id(0) % tiles_per_seq == 0
    zf_ref[0:halo, :] = jnp.where(at_start, 0.0, zh_ref[...])
    zf_ref[halo:, :] = z_ref[...]
    base = halo - (kw - 1)
    n = rc + halo
    for r in range(tm // rc):
        for c in range(d // cc):
            cols = slice(c * cc, (c + 1) * cc)
            blk = zf_ref[r * rc:r * rc + n, cols]
            acc = jnp.broadcast_to(db_ref[:, cols], (rc, cc))
            for s in range(sub):
                sh = blk if s == 0 else pltpu.roll(blk, n - s, 0)
                for q in range(halo // sub + 1):
                    k = sub * q + s - base
                    if 0 <= k < kw:
                        acc = acc + sh[sub * q:sub * q + rc, :] * dw_ref[k:k + 1, cols]
            zc_ref[r * rc:(r + 1) * rc, cols] = acc
    lg = lg_ref[...]
    lb = lb_ref[...]
    for r in range(tm // rl):
        rows = slice(r * rl, (r + 1) * rl)
        zc = zc_ref[rows, :]
        mu = jnp.mean(zc, axis=-1, keepdims=True)
        zc = zc - mu
        var = jnp.mean(zc * zc, axis=-1, keepdims=True)
        y = zc * lax.rsqrt(var + EPS) * lg + lb
        hb_ref[rows, :] = _silu(y).astype(BF16)
    gg = mod_ref[2:3, :] * g_ref[...]
    gs = gn_ref[...] * (1.0 + modn_ref[1:2, :])
    shift = modn_ref[0:1, :]
    ro = 128
    for r in range(tm // ro):
        rows = slice(r * ro, (r + 1) * ro)
        y = _dot(hb_ref[rows, :], w_ref[...])
        xn = _post(y, x_ref[rows, :], gg)
        o_ref[rows, :] = xn
        hn_ref[rows, :] = _modnorm(xn, gs, shift)


def _conv_out(z, dw_w, dw_b, ln_g, ln_b, w, x, mod5, layer, slot, g, *, seq, nxt, tm=None):
    t, d = z.shape
    kw = dw_w.shape[0]
    assert kw - 1 <= CONV_HALO
    tm = tm or _pick(seq, (256, 128))
    tps = seq // tm
    hb = tm // CONV_HALO
    full = lambda i: (0, 0)
    n_in, n_args, n_out, n_shape = _next_specs(nxt, mod5, d, tm, tps, t)
    return pl.pallas_call(
        functools.partial(_conv_out_kernel, tm=tm, kw=kw, rc=tm, cc=LANE, rl=32,
                          tiles_per_seq=tps),
        grid=(t // tm,),
        in_specs=[
            pl.BlockSpec((tm, d), lambda i: (i, 0)),
            pl.BlockSpec((CONV_HALO, d), lambda i: (jnp.maximum(i * hb - 1, 0), 0)),
            pl.BlockSpec((kw, d), full),
            pl.BlockSpec((1, d), full),
            pl.BlockSpec((1, d), full),
            pl.BlockSpec((1, d), full),
            pl.BlockSpec((d, d), full, pipeline_mode=pl.Buffered(1)),
            pl.BlockSpec((tm, d), lambda i: (i, 0)),
            pl.BlockSpec((None, None, None, 3, d), lambda i: (layer, i // tps, slot, 0, 0)),
            pl.BlockSpec((1, d), full),
        ] + n_in,
        out_specs=[pl.BlockSpec((tm, d), lambda i: (i, 0))] + n_out,
        out_shape=[jax.ShapeDtypeStruct((t, d), F32)] + n_shape,
        scratch_shapes=[pltpu.VMEM((tm + CONV_HALO, d), F32), pltpu.VMEM((tm, d), F32),
                        pltpu.VMEM((tm, d), BF16)],
        compiler_params=_cparams(1), name="conv_out",
    )(z, z, dw_w, dw_b.reshape(1, d), ln_g.reshape(1, d), ln_b.reshape(1, d), w, x, mod5,
      g.reshape(1, d), *n_args)


def _split3(a):
    hi = a.astype(BF16)
    r1 = a - hi.astype(F32)
    mid = r1.astype(BF16)
    lo = (r1 - mid.astype(F32)).astype(BF16)
    return hi, mid, lo


def _gla_kernel(q_ref, k_ref, v_ref, r_ref, gl_ref, wg_ref, bg_ref, ng_ref, o_ref, st_ref, b_ref,
                *, heads, dk, dv, tm):
    @pl.when(pl.program_id(1) == 0)
    def _():
        st_ref[...] = jnp.zeros_like(st_ref)

    hk = heads * dk
    ti = lax.broadcasted_iota(jnp.int32, (tm, tm), 0)
    si = lax.broadcasted_iota(jnp.int32, (tm, tm), 1)
    tril = jnp.where(jnp.logical_and(si <= ti, si // CHUNK == ti // CHUNK), 1.0, 0.0).astype(BF16)
    zg = _dot(gl_ref[...], wg_ref[...]) + bg_ref[...]
    log_a = (jnp.minimum(zg, 0.0) - jnp.log1p(jnp.exp(-jnp.abs(zg)))) * (1.0 / GLA_GATE_TEMP)
    for h in range(heads):
        kc = slice(h * dk, (h + 1) * dk)
        parts = _dot(tril, jnp.concatenate(_split3(log_a[:, kc]), axis=1))
        b_ref[:, kc] = parts[:, :dk] + parts[:, dk:2 * dk] + parts[:, 2 * dk:]
    t2 = lax.broadcasted_iota(jnp.int32, (CHUNK, 2 * CHUNK), 0)
    l2 = lax.broadcasted_iota(jnp.int32, (CHUNK, 2 * CHUNK), 1)
    keep_fwd = l2 <= t2
    keep_bwd = l2 - CHUNK > t2
    scale = dk ** -0.5
    ng = ng_ref[...]
    for c in range(tm // CHUNK):
        rows = slice(c * CHUNK, (c + 1) * CHUNK)
        b = b_ref[rows, :]
        b_last = b[CHUNK - 1:CHUNK, :]
        e_pos = jnp.exp(b)
        e_neg = jnp.exp(-b)
        q = q_ref[rows, :].astype(F32) * scale
        k = k_ref[rows, :].astype(F32)
        q_fwd = (q * e_pos).astype(BF16)
        k_fwd = (k * e_neg).astype(BF16)
        q_bwd = (q * e_neg).astype(BF16)
        k_bwd = (k * e_pos).astype(BF16)
        k_state = (k * jnp.exp(b_last - b)).astype(BF16)
        decay = jnp.transpose(jnp.broadcast_to(jnp.exp(b_last), (LANE, hk)))
        for h in range(heads):
            kc = slice(h * dk, (h + 1) * dk)
            vc = slice(h * dv, (h + 1) * dv)
            v = v_ref[rows, vc]
            s2 = _dot_nt(jnp.concatenate([q_fwd[:, kc], q_bwd[:, kc]], axis=0),
                         jnp.concatenate([k_fwd[:, kc], k_bwd[:, kc]], axis=0))
            scores = jnp.where(keep_fwd, s2[:CHUNK, :], jnp.where(keep_bwd, s2[CHUNK:, :], 0.0))
            st = st_ref[h]
            o = _dot(jnp.concatenate([q_fwd[:, kc], scores.astype(BF16)], axis=1),
                     jnp.concatenate([st.astype(BF16), v, v], axis=0))
            st_ref[h] = (st * jnp.concatenate([decay[kc, :]] * (dv // LANE), axis=1)
                         + _dot_tn(k_state[:, kc], v))
            ms = jnp.mean(o * o, axis=-1, keepdims=True)
            o = o * lax.rsqrt(ms + EPS) * ng
            o_ref[rows, vc] = (o * r_ref[rows, vc].astype(F32)).astype(o_ref.dtype)


def _gla_core(qkvr, glow, w_gate, b_gate, norm_g, *, bsz, seq, tm=None):
    heads = GLA_HEADS
    hk = w_gate.shape[1]
    dk = hk // heads
    dv = norm_g.shape[0]
    hv = heads * dv
    assert qkvr.shape[1] == 2 * hk + 2 * hv and hv % hk == 0
    tm = tm or _pick(seq, (256, 128, 64))
    tps = seq // tm
    row = lambda b, t: b * tps + t
    return pl.pallas_call(
        functools.partial(_gla_kernel, heads=heads, dk=dk, dv=dv, tm=tm),
        grid=(bsz, tps),
        in_specs=[
            pl.BlockSpec((tm, hk), lambda b, t: (row(b, t), 0)),
            pl.BlockSpec((tm, hk), lambda b, t: (row(b, t), 1)),
            pl.BlockSpec((tm, hv), lambda b, t: (row(b, t), (2 * hk) // hv)),
            pl.BlockSpec((tm, hv), lambda b, t: (row(b, t), (2 * hk) // hv + 1)),
            pl.BlockSpec((tm, LANE), lambda b, t: (row(b, t), 0)),
            pl.BlockSpec((LANE, hk), lambda b, t: (0, 0)),
            pl.BlockSpec((1, hk), lambda b, t: (0, 0)),
            pl.BlockSpec((1, dv), lambda b, t: (0, 0)),
        ],
        out_specs=pl.BlockSpec((tm, hv), lambda b, t: (row(b, t), 0)),
        out_shape=jax.ShapeDtypeStruct((bsz * seq, hv), BF16),
        scratch_shapes=[pltpu.VMEM((heads, dk, dv), F32), pltpu.VMEM((tm, hk), F32)],
        compiler_params=_cparams(2), name="gla_core",
    )(qkvr, qkvr, qkvr, qkvr, glow, w_gate, b_gate.reshape(1, hk), norm_g.reshape(1, dv))


def _ret_kernel(q_ref, k_ref, v_ref, g_ref, cos_ref, sin_ref, o_ref, st_ref, *, heads, dk, dv, tm):
    @pl.when(pl.program_id(1) == 0)
    def _():
        st_ref[...] = jnp.zeros_like(st_ref)

    half = dk // 2
    ti = lax.broadcasted_iota(jnp.int32, (tm, tm), 0)
    si = lax.broadcasted_iota(jnp.int32, (tm, tm), 1)
    visible = si // CHUNK <= ti // CHUNK
    dist = jnp.abs((ti - si).astype(F32))
    pos = lax.broadcasted_iota(jnp.int32, (tm, LANE), 0).astype(F32)
    kscale = dk ** -0.5
    cos = cos_ref[...]
    sin = sin_ref[...]

    def rot(t):
        t1 = t[:, :half]
        t2 = t[:, half:]
        return jnp.concatenate([t1 * cos - t2 * sin, t1 * sin + t2 * cos], axis=1)

    for h in range(heads):
        log_g = math.log1p(-(2.0 ** (-5.0 - h)))
        weight = jnp.where(visible, jnp.exp(log_g * dist), 0.0)
        q_dec = jnp.concatenate([jnp.exp(log_g * (pos + 1.0))] * (dk // LANE), axis=1)
        k_dec = jnp.concatenate([jnp.exp(log_g * (tm - 1.0 - pos))] * (dk // LANE), axis=1)
        tile_decay = math.exp(log_g * tm)
        kc = slice(h * dk, (h + 1) * dk)
        vc = slice(h * dv, (h + 1) * dv)
        q = rot(q_ref[:, kc].astype(F32))
        k = rot(k_ref[:, kc].astype(F32)) * kscale
        v = v_ref[:, vc]
        scores = _dot_nt(q.astype(BF16), k.astype(BF16)) * weight
        st = st_ref[h]
        o = _dot(jnp.concatenate([scores.astype(BF16), (q * q_dec).astype(BF16)], axis=1),
                 jnp.concatenate([v, st.astype(BF16)], axis=0))
        st_ref[h] = st * tile_decay + _dot_tn((k * k_dec).astype(BF16), v)
        ms = jnp.mean(o * o, axis=-1, keepdims=True)
        o = o * lax.rsqrt(ms + EPS)
        o_ref[:, vc] = (o * g_ref[:, vc].astype(F32)).astype(o_ref.dtype)


def _ret_core(qkvg, *, bsz, seq, tm=None):
    heads = RET_HEADS
    n = qkvg.shape[1]
    hk = n // 6
    hv = 2 * hk
    dk = hk // heads
    dv = hv // heads
    half = dk // 2
    tm = tm or _pick(seq, (256, 128, 64))
    tps = seq // tm
    pos = jnp.arange(seq)
    inv_freq = ROPE_BASE ** (-jnp.arange(half, dtype=F32) / half)
    ang = pos.astype(F32)[:, None] * inv_freq
    cos = jnp.cos(ang)
    sin = jnp.sin(ang)
    row = lambda b, t: b * tps + t
    return pl.pallas_call(
        functools.partial(_ret_kernel, heads=heads, dk=dk, dv=dv, tm=tm),
        grid=(bsz, tps),
        in_specs=[
            pl.BlockSpec((tm, hk), lambda b, t: (row(b, t), 0)),
            pl.BlockSpec((tm, hk), lambda b, t: (row(b, t), 1)),
            pl.BlockSpec((tm, hv), lambda b, t: (row(b, t), 1)),
            pl.BlockSpec((tm, hv), lambda b, t: (row(b, t), 2)),
            pl.BlockSpec((tm, half), lambda b, t: (t, 0)),
            pl.BlockSpec((tm, half), lambda b, t: (t, 0)),
        ],
        out_specs=pl.BlockSpec((tm, hv), lambda b, t: (row(b, t), 0)),
        out_shape=jax.ShapeDtypeStruct((bsz * seq, hv), BF16),
        scratch_shapes=[pltpu.VMEM((heads, dk, dv), F32)],
        compiler_params=_cparams(2), name="ret_core",
    )(qkvg, qkvg, qkvg, qkvg, cos, sin)


def _sgu_kernel(u_ref, v_ref, lg_ref, lb_ref, ws_ref, bs_ref, o_ref, vn_ref, *, heads, hd, tm):
    lg = lg_ref[...]
    lb = lb_ref[...]
    rc = 64
    for r in range(tm // rc):
        rows = slice(r * rc, (r + 1) * rc)
        v = v_ref[rows, :].astype(F32)
        mu = jnp.mean(v, axis=-1, keepdims=True)
        v = v - mu
        var = jnp.mean(v * v, axis=-1, keepdims=True)
        vn_ref[rows, :] = (v * lax.rsqrt(var + EPS) * lg + lb).astype(BF16)
    pi = lax.broadcasted_iota(jnp.int32, (SGU_CHUNK, SGU_CHUNK), 0)
    pj = lax.broadcasted_iota(jnp.int32, (SGU_CHUNK, SGU_CHUNK), 1)
    mask = (pj // CHUNK) <= (pi // CHUNK)
    for h in range(heads):
        w = jnp.where(mask, ws_ref[h], 0.0).astype(BF16)
        bias = bs_ref[h]
        cols = slice(h * hd, (h + 1) * hd)
        for n in range(tm // SGU_CHUNK):
            rows = slice(n * SGU_CHUNK, (n + 1) * SGU_CHUNK)
            sv = _dot(w, vn_ref[rows, cols]) + bias
            o_ref[rows, cols] = (u_ref[rows, cols].astype(F32) * sv).astype(o_ref.dtype)


def _sgu_core(uv, ln_g, ln_b, w_s, b_s, *, seq, tm=None):
    t = uv.shape[0]
    width = uv.shape[1] // 2
    heads = SGU_HEADS
    hd = width // heads
    tm = tm or _pick(seq, (256, 128))
    full2 = lambda i: (0, 0)
    full3 = lambda i: (0, 0, 0)
    return pl.pallas_call(
        functools.partial(_sgu_kernel, heads=heads, hd=hd, tm=tm),
        grid=(t // tm,),
        in_specs=[
            pl.BlockSpec((tm, width), lambda i: (i, 0)),
            pl.BlockSpec((tm, width), lambda i: (i, 1)),
            pl.BlockSpec((1, width), full2),
            pl.BlockSpec((1, width), full2),
            pl.BlockSpec((heads, SGU_CHUNK, SGU_CHUNK), full3),
            pl.BlockSpec((heads, SGU_CHUNK, 1), full3),
        ],
        out_specs=pl.BlockSpec((tm, width), lambda i: (i, 0)),
        out_shape=jax.ShapeDtypeStruct((t, width), BF16),
        scratch_shapes=[pltpu.VMEM((tm, width), BF16)],
        compiler_params=_cparams(1), name="sgu_core",
    )(uv, uv, ln_g.reshape(1, width), ln_b.reshape(1, width), w_s,
      b_s.reshape(heads, SGU_CHUNK, 1))


def kernel(x, c, ada_w, ada_b, norm_g, ffn_w_in, ffn_conv_w, ffn_conv_b, ffn_w_out, cm_w_in, cm_dw_w, cm_dw_b, cm_ln_g, cm_ln_b, cm_w_out, gla_w_in, gla_w_gate, gla_b_gate, gla_norm_g, gla_w_out, ret_w_in, ret_w_out, sgu_w_in, sgu_ln_g, sgu_ln_b, sgu_w_s, sgu_b_s, sgu_w_out):
    bsz, seq, d = x.shape
    depth = ada_w.shape[0]
    mod5 = _ada(c, ada_w, ada_b).reshape(depth, bsz, 2, 3, d)
    xf = x.reshape(bsz * seq, d)
    bf = lambda a: a.astype(BF16)
    h = _modnorm_call(xf, mod5, 0, 0, norm_g[0, 0], seq=seq)
    for i in range(depth):
        m, j = i % N_MIXERS, i // N_MIXERS
        mm_in = functools.partial(_mm_in, h, seq=seq)
        post = dict(x=xf, mod5=mod5, layer=i, slot=0, g=norm_g[i, 1], seq=seq,
                    nxt=(i, 1, norm_g[i, 2]))
        if m == 0:
            z = mm_in(cm_w_in, j, cm_w_in.shape[2] // 2, kind="glu", out_dtype=F32)
            xf, h = _conv_out(z, cm_dw_w[j], cm_dw_b[j], cm_ln_g[j], cm_ln_b[j], bf(cm_w_out[j]), **post)
        elif m == 1:
            rank, hk = gla_w_gate.shape[1:]
            n_main = gla_w_in.shape[2] - rank
            hv = (n_main - 2 * hk) // 2
            w_low = jnp.pad(gla_w_in[j][:, n_main:], ((0, 0), (0, LANE - rank)))
            w_gate = jnp.pad(gla_w_gate[j], ((0, LANE - rank), (0, 0)))
            qkvr = mm_in(gla_w_in, j, n_main, kind="plain", out_dtype=BF16, silu_cols=2 * hk + hv)
            glow = mm_in(w_low[None], 0, LANE, kind="plain", out_dtype=BF16)
            o = _gla_core(qkvr, glow, bf(w_gate), gla_b_gate[j], gla_norm_g[j], bsz=bsz, seq=seq)
            xf, h = _mm_out(o, bf(gla_w_out[j]), **post)
        elif m == 2:
            n_ret = ret_w_in.shape[2]
            qkvg = mm_in(ret_w_in, j, n_ret, kind="plain", out_dtype=BF16, silu_cols=n_ret - n_ret // 3)
            o = _ret_core(qkvg, bsz=bsz, seq=seq)
            xf, h = _mm_out(o, bf(ret_w_out[j]), **post)
        else:
            uv = mm_in(sgu_w_in, j, sgu_w_in.shape[2], kind="gelu", out_dtype=BF16)
            o = _sgu_core(uv, sgu_ln_g[j], sgu_ln_b[j], sgu_w_s[j], sgu_b_s[j], seq=seq)
            xf, h = _mm_out(o, bf(sgu_w_out[j]), **post)
        u = _mm_in(h, ffn_w_in, i, ffn_w_in.shape[2] // 2, seq=seq, kind="convffn",
                   out_dtype=BF16, conv_w=ffn_conv_w, conv_b=ffn_conv_b)
        nxt = (i + 1, 0, norm_g[i + 1, 0]) if i + 1 < depth else None
        res = _mm_out(u, bf(ffn_w_out[i]), xf, mod5, i, 1, norm_g[i, 3], seq=seq, nxt=nxt)
        xf, h = res if nxt is not None else (res, None)
    return xf.reshape(bsz, seq, d)
```
